```python
import jax
import jax.numpy as jnp
from jax import lax
import numpy as np

D_MODEL = 2048
BATCH = 32
SEQ = 256
DEPTH = 4
DEC_BATCH = 8
DEC_SEQ = 4096
PAST_LEN = 512

GRID_W = 64
N_EVEN = (DEPTH + 1) // 2
N_ODD = DEPTH // 2
EPS = 1e-6
MLA_HEADS = 8
QK_NOPE = 128
QK_ROPE = 64
V_HEAD = 128
Q_LORA = 512
KV_LORA = 256
ROPE_THETA = 10000.0
ROPE_FREQ = QK_ROPE // 4
Q_BLOCK = 128
CONV_W = D_MODEL // 2
CONV_K = 3
EV_IN = Q_LORA + KV_LORA + QK_ROPE + 3 * CONV_W
EV_MIX = MLA_HEADS * V_HEAD + CONV_W
EV_SPLITS = (Q_LORA, Q_LORA + KV_LORA, Q_LORA + KV_LORA + QK_ROPE,
             Q_LORA + KV_LORA + QK_ROPE + CONV_W, Q_LORA + KV_LORA + QK_ROPE + 2 * CONV_W)
GLA_HEADS = 4
GLA_DK = D_MODEL // 2
GLA_DV = D_MODEL
GLA_DKH = GLA_DK // GLA_HEADS
GLA_DVH = GLA_DV // GLA_HEADS
GATE_RANK = 16
GATE_NORM = 16.0
GLA_CHUNK = 64
OD_IN = 2 * GLA_DK + 2 * GLA_DV + 2 * GATE_RANK
OD_SPLITS = (GLA_DK, 2 * GLA_DK, 2 * GLA_DK + GLA_DV, 2 * GLA_DK + 2 * GLA_DV,
             2 * GLA_DK + 2 * GLA_DV + GATE_RANK)
N_EXPERTS = 32
TOP_K = 4
D_FF = D_MODEL
SWIGLU_ALPHA = 1.702
SWIGLU_LIMIT = 7.0
MOE_BLOCK = 256

kernel_name = 'hybrid_mla_conv_gla_moe_denoise_step'


def rmsnorm(x, w):
    xf = x.astype(jnp.float32)
    y = xf * lax.rsqrt(jnp.mean(xf * xf, axis=-1, keepdims=True) + EPS)
    return (y * w.astype(jnp.float32)).astype(x.dtype)


def adaln(cond, w, b):
    mod = jax.nn.silu(cond) @ w + b
    return [m[:, None, :] for m in jnp.split(mod, 6, axis=-1)]


def modulate(h, shift, scale):
    return h * (1 + scale) + shift


def grid_rope(n_tokens):
    rows = n_tokens // GRID_W
    row = jnp.repeat(jnp.arange(rows), GRID_W).astype(jnp.float32)
    col = jnp.tile(jnp.arange(GRID_W), rows).astype(jnp.float32)
    inv = jnp.power(ROPE_THETA, -jnp.arange(ROPE_FREQ, dtype=jnp.float32) / ROPE_FREQ)
    ang = jnp.stack([row[:, None] * inv, col[:, None] * inv], axis=1)
    return jnp.cos(ang)[:, None], jnp.sin(ang)[:, None]


def apply_rope_2d(x, cos, sin):
    xs = x.reshape(x.shape[:-1] + (2, 2, ROPE_FREQ)).astype(jnp.float32)
    x1, x2 = xs[..., 0, :], xs[..., 1, :]
    out = jnp.stack([x1 * cos - x2 * sin, x2 * cos + x1 * sin], axis=-2)
    return out.reshape(x.shape).astype(x.dtype)


def block_attention(q, k, v):
    b_, lq, h, dh = q.shape
    nb = lq // Q_BLOCK
    scale = dh ** -0.5
    qb = q.reshape(b_, nb, Q_BLOCK, h, dh).swapaxes(0, 1)

    def one_block(qi):
        s = jnp.einsum('bqhd,bkhd->bhqk', qi, k).astype(jnp.float32) * scale
        p = jax.nn.softmax(s, axis=-1).astype(v.dtype)
        return jnp.einsum('bhqk,bkhv->bqhv', p, v)

    o = lax.map(one_block, qb)
    return o.swapaxes(0, 1).reshape(b_, lq, h, v.shape[-1])


def short_conv(u, w):
    up = jnp.pad(u, ((0, 0), (1, 1), (0, 0)))
    return up[:, :-2] * w[0] + up[:, 1:-1] * w[1] + up[:, 2:] * w[2]


def mla_conv_inputs(h, p):
    b_, l_, _ = h.shape
    z = h @ p['w_in']
    q_a, kv_a, k_r, b_g, c_g, x_v = jnp.split(z, EV_SPLITS, axis=-1)
    q = (rmsnorm(q_a, p['q_a_norm']) @ p['w_qb']).reshape(b_, l_, MLA_HEADS, QK_NOPE + QK_ROPE)
    q_nope = rmsnorm(q[..., :QK_NOPE], p['q_norm'])
    q_rope = rmsnorm(q[..., QK_NOPE:], p['q_rope_norm'])
    ckv = rmsnorm(kv_a, p['kv_a_norm'])
    k_rope = rmsnorm(k_r, p['k_rope_norm'])[:, :, None, :]
    conv = b_g * short_conv(c_g * x_v, p['conv_w'])
    return q_nope, q_rope, ckv, k_rope, conv


def mla_keys(ckv, k_rope, p):
    b_, l_, _ = ckv.shape
    kv = (ckv @ p['w_kvb']).reshape(b_, l_, MLA_HEADS, QK_NOPE + V_HEAD)
    k_nope = rmsnorm(kv[..., :QK_NOPE], p['k_norm'])
    k_r = jnp.broadcast_to(k_rope, (b_, l_, MLA_HEADS, QK_ROPE)).astype(k_nope.dtype)
    return jnp.concatenate([k_nope, k_r], axis=-1), kv[..., QK_NOPE:]


def mix_out(o, conv, p):
    b_, l_ = o.shape[:2]
    return jnp.concatenate([o.reshape(b_, l_, MLA_HEADS * V_HEAD), conv], axis=-1) @ p['w_out']


def even_mixer_context(h, p):
    q_nope, q_rope, ckv, k_rope, conv = mla_conv_inputs(h, p)
    k, v = mla_keys(ckv, k_rope, p)
    o = block_attention(jnp.concatenate([q_nope, q_rope], axis=-1), k, v)
    return mix_out(o, conv, p), ckv, k_rope[:, :, 0, :]


def even_mixer_latent(h, ckv_ctx, krope_ctx, p):
    q_nope, q_rope, ckv, k_rope, conv = mla_conv_inputs(h, p)
    cos, sin = grid_rope(h.shape[1])
    q = jnp.concatenate([q_nope, apply_rope_2d(q_rope, cos, sin)], axis=-1)
    k_lat, v_lat = mla_keys(ckv, apply_rope_2d(k_rope, cos, sin), p)
    k_ctx, v_ctx = mla_keys(ckv_ctx, krope_ctx[:, :, None, :], p)
    o = block_attention(q, jnp.concatenate([k_ctx, k_lat], axis=1), jnp.concatenate([v_ctx, v_lat], axis=1))
    return mix_out(o, conv, p)


def gla_inputs(h, p):
    b_, l_, _ = h.shape
    z = h @ p['w_in']
    q, k, v, g, r_f, r_b = jnp.split(z, OD_SPLITS, axis=-1)
    q = q.reshape(b_, l_, GLA_HEADS, GLA_DKH) * (GLA_DKH ** -0.5)
    k = k.reshape(b_, l_, GLA_HEADS, GLA_DKH)
    v = v.reshape(b_, l_, GLA_HEADS, GLA_DVH)

    def log_decay(r, d):
        pre = (r @ p['w_gk'][d] + p['b_gk'][d]).astype(jnp.float32)
        return (jax.nn.log_sigmoid(pre) / GATE_NORM).reshape(b_, l_, GLA_HEADS, GLA_DKH)

    return q, k, v, g, log_decay(r_f, 0), log_decay(r_b, 1)


def gla_scan(q, k, v, log_a, s0):
    b_, l_, h, dk = q.shape
    dv = v.shape[-1]
    n = l_ // GLA_CHUNK

    def chunks(t):
        return t.astype(jnp.float32).reshape(b_, n, GLA_CHUNK, h, t.shape[-1]).swapaxes(0, 1)

    causal = jnp.tril(jnp.ones((GLA_CHUNK, GLA_CHUNK), dtype=bool))

    def step(s, inp):
        qc, kc, vc, ac = inp
        b = jnp.cumsum(ac, axis=1)
        b_last = b[:, -1]
        q_dec = qc * jnp.exp(b)
        k_dec = kc * jnp.exp(-b)
        scores = jnp.where(causal, jnp.einsum('bihd,bjhd->bhij', q_dec, k_dec), 0.0)
        o = jnp.einsum('bhij,bjhv->bihv', scores, vc) + jnp.einsum('bihd,bhdv->bihv', q_dec, s)
        k_end = kc * jnp.exp(b_last[:, None] - b)
        s_new = jnp.exp(b_last)[..., None] * s + jnp.einsum('bjhd,bjhv->bhdv', k_end, vc)
        return s_new, o

    s_final, o = lax.scan(step, s0.astype(jnp.float32), (chunks(q), chunks(k), chunks(v), chunks(log_a)))
    return o.swapaxes(0, 1).reshape(b_, l_, h, dv).astype(v.dtype), s_final


def gla_bidir(q, k, v, la_f, la_b, s_f0, s_b0):
    o_f, s_f = gla_scan(q, k, v, la_f, s_f0)
    o_b, s_b = gla_scan(jnp.flip(q, 1), jnp.flip(k, 1), jnp.flip(v, 1), jnp.flip(la_b, 1), s_b0)
    return o_f + jnp.flip(o_b, 1), s_f, s_b


def gla_output(o, g, p):
    b_, l_ = o.shape[:2]
    o = rmsnorm(o, p['o_norm']).reshape(b_, l_, GLA_DV)
    return (o * jax.nn.silu(g)) @ p['w_out']


def odd_mixer_context(h, p):
    q, k, v, g, la_f, la_b = gla_inputs(h, p)
    zeros = jnp.zeros((h.shape[0], GLA_HEADS, GLA_DKH, GLA_DVH), jnp.float32)
    o, s_f, s_b = gla_bidir(q, k, v, la_f, la_b, zeros, zeros)
    return gla_output(o, g, p), jnp.stack([s_f, s_b], axis=1).astype(h.dtype)


def odd_mixer_latent(h, state_ctx, p):
    q, k, v, g, la_f, la_b = gla_inputs(h, p)
    o, _, _ = gla_bidir(q, k, v, la_f, la_b, state_ctx[:, 0], state_ctx[:, 1])
    return gla_output(o, g, p)


def moe_ffn(x2d, w_router, b_router, w_gate, b_gate, w_up, b_up, w_down, b_down):
    t_, d_ = x2d.shape
    n_assign = t_ * TOP_K
    logits = (x2d @ w_router + b_router).astype(jnp.float32)
    top_logit, top_e = lax.top_k(logits, TOP_K)
    weights = jax.nn.softmax(top_logit, axis=-1)
    flat_e = top_e.reshape(-1)
    order = jnp.argsort(flat_e)
    e_sorted = flat_e[order]
    tok_sorted = (order // TOP_K).astype(jnp.int32)
    w_sorted = weights.reshape(-1)[order]
    counts = jnp.bincount(flat_e, length=N_EXPERTS)
    padded = (counts + MOE_BLOCK - 1) // MOE_BLOCK * MOE_BLOCK
    pad_end = jnp.cumsum(padded)
    pad_start = pad_end - padded
    start = jnp.cumsum(counts) - counts
    dest = pad_start[e_sorted] + jnp.arange(n_assign) - start[e_sorted]
    n_blocks = -(-n_assign // MOE_BLOCK) + N_EXPERTS
    cap = n_blocks * MOE_BLOCK
    src = jnp.full((cap,), t_, jnp.int32).at[dest].set(tok_sorted)
    row_w = jnp.zeros((cap,), jnp.float32).at[dest].set(w_sorted)
    x_ext = jnp.concatenate([x2d, jnp.zeros((1, d_), x2d.dtype)], axis=0)
    xb = x_ext[src].reshape(n_blocks, MOE_BLOCK, d_)
    block_e = jnp.minimum(jnp.searchsorted(pad_end, jnp.arange(n_blocks) * MOE_BLOCK, side='right'), N_EXPERTS - 1)

    def expert_block(args):
        xi, e = args
        gate = jnp.minimum(xi @ w_gate[e] + b_gate[e], SWIGLU_LIMIT)
        up = jnp.clip(xi @ w_up[e] + b_up[e], -SWIGLU_LIMIT, SWIGLU_LIMIT)
        hdn = (up + 1) * (gate * jax.nn.sigmoid(SWIGLU_ALPHA * gate))
        return hdn @ w_down[e] + b_down[e]

    yb = lax.map(expert_block, (xb, block_e)).reshape(cap, d_)
    y = jnp.zeros((t_ + 1, d_), jnp.float32).at[src].add(yb.astype(jnp.float32) * row_w[:, None])
    return y[:t_].astype(x2d.dtype)


def setup_inputs(seed: int = 0) -> dict:
    key = jax.random.key(seed)
    ks = jax.random.split(key, 48)
    cnt = [0]

    def nrm(shape, scale):
        k = ks[cnt[0]]
        cnt[0] += 1
        return scale * jax.random.normal(k, shape, jnp.float32)

    def gain(shape):
        return 1.0 + nrm(shape, 0.02)

    D = D_MODEL
    return {
        'x_prompt': nrm((BATCH, SEQ, D), 1.0),
        'x_sample': nrm((DEC_BATCH, DEC_SEQ, D), 1.0),
        'cache_mla_ckv': nrm((DEC_BATCH, N_EVEN, PAST_LEN, KV_LORA), 1.0),
        'cache_mla_krope': nrm((DEC_BATCH, N_EVEN, PAST_LEN, QK_ROPE), 1.0),
        'state_gla': nrm((DEC_BATCH, N_ODD, 2, GLA_HEADS, GLA_DKH, GLA_DVH), 0.5),
        'c': nrm((DEC_BATCH, D), 1.0),
        'c_ctx': nrm((D,), 1.0),
        'ada_w': nrm((DEPTH, D, 6 * D), 0.5 * D ** -0.5),
        'ada_b': nrm((DEPTH, 6 * D), 0.02),
        'norm_mix_w': gain((DEPTH, D)),
        'norm_ffn_w': gain((DEPTH, D)),
        'ev_w_in': nrm((N_EVEN, D, EV_IN), D ** -0.5),
        'ev_q_a_norm': gain((N_EVEN, Q_LORA)),
        'ev_w_qb': nrm((N_EVEN, Q_LORA, MLA_HEADS * (QK_NOPE + QK_ROPE)), Q_LORA ** -0.5),
        'ev_kv_a_norm': gain((N_EVEN, KV_LORA)),
        'ev_w_kvb': nrm((N_EVEN, KV_LORA, MLA_HEADS * (QK_NOPE + V_HEAD)), KV_LORA ** -0.5),
        'ev_q_norm': gain((N_EVEN, QK_NOPE)),
        'ev_k_norm': gain((N_EVEN, QK_NOPE)),
        'ev_q_rope_norm': gain((N_EVEN, QK_ROPE)),
        'ev_k_rope_norm': gain((N_EVEN, QK_ROPE)),
        'ev_conv_w': nrm((N_EVEN, CONV_K, CONV_W), CONV_K ** -0.5),
        'ev_w_out': nrm((N_EVEN, EV_MIX, D), EV_MIX ** -0.5),
        'od_w_in': nrm((N_ODD, D, OD_IN), D ** -0.5),
        'od_w_gk': nrm((N_ODD, 2, GATE_RANK, GLA_DK), GATE_RANK ** -0.5),
        'od_b_gk': nrm((N_ODD, 2, GLA_DK), 0.1),
        'od_o_norm': gain((N_ODD, GLA_DVH)),
        'od_w_out': nrm((N_ODD, GLA_DV, D), GLA_DV ** -0.5),
        'moe_w_router': nrm((DEPTH, D, N_EXPERTS), D ** -0.5),
        'moe_b_router': nrm((DEPTH, N_EXPERTS), 0.01),
        'moe_w_gate': nrm((DEPTH, N_EXPERTS, D, D_FF), D ** -0.5),
        'moe_b_gate': nrm((DEPTH, N_EXPERTS, D_FF), 0.02),
        'moe_w_up': nrm((DEPTH, N_EXPERTS, D, D_FF), D ** -0.5),
        'moe_b_up': nrm((DEPTH, N_EXPERTS, D_FF), 0.02),
        'moe_w_down': nrm((DEPTH, N_EXPERTS, D_FF, D), D_FF ** -0.5),
        'moe_b_down': nrm((DEPTH, N_EXPERTS, D), 0.02),
    }


def reference(x_prompt, x_sample, cache_mla_ckv, cache_mla_krope, state_gla, c, c_ctx,
              ada_w, ada_b, norm_mix_w, norm_ffn_w,
              ev_w_in, ev_q_a_norm, ev_w_qb, ev_kv_a_norm, ev_w_kvb, ev_q_norm, ev_k_norm,
              ev_q_rope_norm, ev_k_rope_norm, ev_conv_w, ev_w_out,
              od_w_in, od_w_gk, od_b_gk, od_o_norm, od_w_out,
              moe_w_router, moe_b_router, moe_w_gate, moe_b_gate, moe_w_up, moe_b_up,
              moe_w_down, moe_b_down):
    xp, xs = x_prompt, x_sample
    n_ctx_tok = xp.shape[0] * xp.shape[1]
    new_ckv, new_krope, new_gla = [], [], []
    for l in range(DEPTH):
        mod_p = adaln(c_ctx[None, :], ada_w[l], ada_b[l])
        mod_s = adaln(c, ada_w[l], ada_b[l])
        hp = modulate(rmsnorm(xp, norm_mix_w[l]), mod_p[0], mod_p[1])
        hs = modulate(rmsnorm(xs, norm_mix_w[l]), mod_s[0], mod_s[1])
        i = l // 2
        if l % 2 == 0:
            p = {'w_in': ev_w_in[i], 'q_a_norm': ev_q_a_norm[i], 'w_qb': ev_w_qb[i],
                 'kv_a_norm': ev_kv_a_norm[i], 'w_kvb': ev_w_kvb[i], 'q_norm': ev_q_norm[i],
                 'k_norm': ev_k_norm[i], 'q_rope_norm': ev_q_rope_norm[i],
                 'k_rope_norm': ev_k_rope_norm[i], 'conv_w': ev_conv_w[i], 'w_out': ev_w_out[i]}
            op, ckv, krope = even_mixer_context(hp, p)
            os_ = even_mixer_latent(hs, cache_mla_ckv[:, i], cache_mla_krope[:, i], p)
            new_ckv.append(ckv)
            new_krope.append(krope)
        else:
            p = {'w_in': od_w_in[i], 'w_gk': od_w_gk[i], 'b_gk': od_b_gk[i],
                 'o_norm': od_o_norm[i], 'w_out': od_w_out[i]}
            op, st = odd_mixer_context(hp, p)
            os_ = odd_mixer_latent(hs, state_gla[:, i], p)
            new_gla.append(st)
        xp = xp + mod_p[2] * op
        xs = xs + mod_s[2] * os_
        hp = modulate(rmsnorm(xp, norm_ffn_w[l]), mod_p[3], mod_p[4])
        hs = modulate(rmsnorm(xs, norm_ffn_w[l]), mod_s[3], mod_s[4])
        tokens = jnp.concatenate([hp.reshape(-1, D_MODEL), hs.reshape(-1, D_MODEL)], axis=0)
        f = moe_ffn(tokens, moe_w_router[l], moe_b_router[l], moe_w_gate[l], moe_b_gate[l],
                    moe_w_up[l], moe_b_up[l], moe_w_down[l], moe_b_down[l])
        xp = xp + mod_p[5] * f[:n_ctx_tok].reshape(xp.shape)
        xs = xs + mod_s[5] * f[n_ctx_tok:].reshape(xs.shape)
    new_mla_ckv = jnp.stack(new_ckv, axis=1)
    new_mla_krope = jnp.stack(new_krope, axis=1)
    new_gla_state = jnp.stack(new_gla, axis=1)
    return (xp, xs, new_mla_ckv, new_mla_krope, new_gla_state)
```

```python
import functools

import jax
import jax.numpy as jnp
import numpy as np
from jax import lax
from jax.experimental import pallas as pl
from jax.experimental.pallas import tpu as pltpu
from jax.experimental.pallas import tpu_sc as plsc

F32 = jnp.float32
BF16 = jnp.bfloat16
I32 = jnp.int32
U32 = jnp.uint32

EPS = 1e-6
GRID_W = 64
MLA_HEADS = 8
QK_NOPE = 128
QK_ROPE = 64
V_HEAD = 128
Q_LORA = 512
KV_LORA = 256
ROPE_THETA = 10000.0
ROPE_FREQ = QK_ROPE // 4
CONV_W = 1024
GLA_HEADS = 4
GLA_DKH = 256
GLA_DVH = 512
GATE_RANK = 16
GATE_NORM = 16.0
GLA_CHUNK = 64
TOP_K = 4
SWIGLU_ALPHA = 1.702
SWIGLU_LIMIT = 7.0

V7X_VMEM_LIMIT = 56 * 1024 * 1024
LANES = 128
HI = lax.Precision.HIGHEST


def _cparams(sem):
    return pltpu.CompilerParams(dimension_semantics=sem, vmem_limit_bytes=V7X_VMEM_LIMIT)


def _pick(want, *dims):
    t = want
    while any(d % t for d in dims):
        t //= 2
    return t


def _dot(a, b):
    return jnp.dot(a, b, preferred_element_type=F32)


def _rms(x, w):
    return x * lax.rsqrt(jnp.mean(x * x, axis=-1, keepdims=True) + EPS) * w


def _lnmod(x, nw, mod, ks):
    return _rms(x, nw) * (1.0 + mod[ks + 1:ks + 2, :]) + mod[ks:ks + 1, :]


class _Geo:
    def __init__(self, x_prompt, x_sample, cache_ckv):
        self.n_ctx, self.seq_ctx, self.d = x_prompt.shape
        self.n_lat, self.seq_lat, _ = x_sample.shape
        self.past = cache_ckv.shape[2]
        self.t_ctx = self.n_ctx * self.seq_ctx
        self.t_lat = self.n_lat * self.seq_lat
        self.t = self.t_ctx + self.t_lat

    def mod_row(self, tm):
        t_ctx, seq_lat = self.t_ctx, self.seq_lat
        return lambda i: jnp.where(i * tm < t_ctx, 0, 1 + (i * tm - t_ctx) // seq_lat)


def _adaln_kernel(c_ref, w_ref, b_ref, o_ref):
    c = c_ref[...]
    a = (c * jax.nn.sigmoid(c)).astype(BF16)
    o_ref[0] = _dot(a, w_ref[0].astype(BF16)) + b_ref[0]


def _adaln(cond16, ada_w, ada_b):
    depth, d, n = ada_w.shape
    tn = 1024
    return pl.pallas_call(
        _adaln_kernel,
        grid=(depth, n // tn),
        in_specs=[
            pl.BlockSpec((16, d), lambda l, j: (0, 0)),
            pl.BlockSpec((1, d, tn), lambda l, j: (l, 0, j)),
            pl.BlockSpec((1, 1, tn), lambda l, j: (l, 0, j)),
        ],
        out_specs=pl.BlockSpec((1, 16, tn), lambda l, j: (l, 0, j)),
        out_shape=jax.ShapeDtypeStruct((depth, 16, n), F32),
        compiler_params=_cparams(("parallel", "parallel")),
        name="adaln",
    )(cond16, ada_w, ada_b.reshape(depth, 1, n))


def _lnmod_mm_kernel(x_ref, nw_ref, mod_ref, w_ref, o_ref, h_scr, *, ks):
    @pl.when(pl.program_id(1) == 0)
    def _():
        h_scr[...] = _lnmod(x_ref[...], nw_ref[...], mod_ref[0], ks).astype(BF16)

    o_ref[...] = _dot(h_scr[...], w_ref[...])


def _lnmod_mm(geo, x, nw, mod, w, ks):
    t, d = x.shape
    n = w.shape[1]
    tm = _pick(1024, geo.t_ctx, geo.seq_lat)
    tn = 512
    row = geo.mod_row(tm)
    return pl.pallas_call(
        functools.partial(_lnmod_mm_kernel, ks=ks),
        grid=(t // tm, pl.cdiv(n, tn)),
        in_specs=[
            pl.BlockSpec((tm, d), lambda i, j: (i, 0)),
            pl.BlockSpec((1, d), lambda i, j: (0, 0)),
            pl.BlockSpec((1, 6, d), lambda i, j: (row(i), 0, 0)),
            pl.BlockSpec((d, tn), lambda i, j: (0, j)),
        ],
        out_specs=pl.BlockSpec((tm, tn), lambda i, j: (i, j)),
        out_shape=jax.ShapeDtypeStruct((t, n), F32),
        scratch_shapes=[pltpu.VMEM((tm, d), BF16)],
        compiler_params=_cparams(("parallel", "arbitrary")),
        name="lnmod_mm",
    )(x, nw.reshape(1, d), mod, w)


EV_QA = 3 * CONV_W
EV_KVA = EV_QA + Q_LORA
EV_KR = EV_KVA + KV_LORA
EV_N = EV_KR + 2 * QK_ROPE


def _even_mid_kernel(bg_ref, cg_ref, xv_ref, cgp_ref, xvp_ref, cgn_ref, xvn_ref, qa_ref, kva_ref, kr_ref,
                     cw_ref, qaw_ref, kvw_ref, krw_ref, tab_ref,
                     conv_ref, qan_ref, ckv_ref, krn_ref, krot_ref, *, t_ctx, seq_ctx, seq_lat):
    i = pl.program_id(0)
    tm = bg_ref.shape[0]
    u = cg_ref[...] * xv_ref[...]
    rows = lax.broadcasted_iota(I32, (tm, 1), 0)
    tok = i * tm + rows
    pos = jnp.where(tok < t_ctx, tok & (seq_ctx - 1), (tok - t_ctx) & (seq_lat - 1))
    last = jnp.where(tok < t_ctx, seq_ctx - 1, seq_lat - 1)
    u_m1 = jnp.where(rows == 0, cgp_ref[7:8, :] * xvp_ref[7:8, :], pltpu.roll(u, 1, axis=0))
    u_p1 = jnp.where(rows == tm - 1, cgn_ref[0:1, :] * xvn_ref[0:1, :], pltpu.roll(u, tm - 1, axis=0))
    u_m1 = jnp.where(pos == 0, 0.0, u_m1)
    u_p1 = jnp.where(pos == last, 0.0, u_p1)
    cw = cw_ref[...]
    conv = bg_ref[...] * (u_m1 * cw[0:1, :] + u * cw[1:2, :] + u_p1 * cw[2:3, :])
    conv_ref[...] = conv.astype(BF16)
    qan_ref[...] = _rms(qa_ref[...], qaw_ref[...]).astype(BF16)
    ckv_ref[...] = _rms(kva_ref[...], kvw_ref[...])
    kr = kr_ref[...]
    krn = _rms(kr, krw_ref[...])
    krn_ref[...] = krn[:, :QK_ROPE]
    y = krn * tab_ref[...]
    krot_ref[...] = (y + pltpu.roll(y, QK_ROPE, axis=1)).astype(BF16)


def _even_mid(geo, z, conv_w, qa_w, kv_w, kr_w128, tab, tq):
    t = z.shape[0]
    tm = tq
    ncb = geo.t_ctx // tm
    bps = geo.seq_lat // tm
    nb8 = t // 8
    r8 = tm // 8
    assert geo.seq_ctx & (geo.seq_ctx - 1) == 0 and geo.seq_lat & (geo.seq_lat - 1) == 0
    tab_idx = lambda i: jnp.where(i < ncb, 0, 1 + (i - ncb) % bps)
    col = lambda c: (lambda i: (i, c))
    in_specs = [
        pl.BlockSpec((tm, CONV_W), col(0)),
        pl.BlockSpec((tm, CONV_W), col(1)),
        pl.BlockSpec((tm, CONV_W), col(2)),
        pl.BlockSpec((8, CONV_W), lambda i: (jnp.maximum(i * r8 - 1, 0), 1)),
        pl.BlockSpec((8, CONV_W), lambda i: (jnp.maximum(i * r8 - 1, 0), 2)),
        pl.BlockSpec((8, CONV_W), lambda i: (jnp.minimum((i + 1) * r8, nb8 - 1), 1)),
        pl.BlockSpec((8, CONV_W), lambda i: (jnp.minimum((i + 1) * r8, nb8 - 1), 2)),
        pl.BlockSpec((tm, Q_LORA), col(EV_QA // Q_LORA)),
        pl.BlockSpec((tm, KV_LORA), col(EV_KVA // KV_LORA)),
        pl.BlockSpec((tm, 2 * QK_ROPE), col(EV_KR // (2 * QK_ROPE))),
        pl.BlockSpec((3, CONV_W), lambda i: (0, 0)),
        pl.BlockSpec((1, Q_LORA), lambda i: (0, 0)),
        pl.BlockSpec((1, KV_LORA), lambda i: (0, 0)),
        pl.BlockSpec((1, 2 * QK_ROPE), lambda i: (0, 0)),
        pl.BlockSpec((tm, 2 * QK_ROPE), lambda i: (tab_idx(i), 0)),
    ]
    out_specs = [
        pl.BlockSpec((tm, CONV_W), lambda i: (i, 0)),
        pl.BlockSpec((tm, Q_LORA), lambda i: (i, 0)),
        pl.BlockSpec((tm, KV_LORA), lambda i: (i, 0)),
        pl.BlockSpec((tm, QK_ROPE), lambda i: (i, 0)),
        pl.BlockSpec((tm, 2 * QK_ROPE), lambda i: (i, 0)),
    ]
    out_shape = [
        jax.ShapeDtypeStruct((t, CONV_W), BF16),
        jax.ShapeDtypeStruct((t, Q_LORA), BF16),
        jax.ShapeDtypeStruct((t, KV_LORA), F32),
        jax.ShapeDtypeStruct((t, QK_ROPE), F32),
        jax.ShapeDtypeStruct((t, 2 * QK_ROPE), BF16),
    ]
    return pl.pallas_call(
        functools.partial(_even_mid_kernel, t_ctx=geo.t_ctx, seq_ctx=geo.seq_ctx, seq_lat=geo.seq_lat),
        grid=(t // tm,),
        in_specs=in_specs,
        out_specs=out_specs,
        out_shape=out_shape,
        compiler_params=_cparams(("parallel",)),
        name="even_mid",
    )(z, z, z, z, z, z, z, z, z, z, conv_w, qa_w.reshape(1, -1), kv_w.reshape(1, -1), kr_w128, tab)


def _q_proj_kernel(a_ref, w_ref, nw_ref, tab_ref, o_ref):
    acc = _dot(a_ref[...], w_ref[...])
    nw = nw_ref[...]
    tab = tab_ref[...]
    for h in range(MLA_HEADS):
        n = acc[:, h * 256:h * 256 + QK_NOPE]
        r = acc[:, h * 256 + QK_NOPE:(h + 1) * 256]
        qn = _rms(n, nw[:, :QK_NOPE])
        y = _rms(r, nw[:, QK_NOPE:]) * tab
        o_ref[h] = jnp.concatenate([qn, y], axis=-1).astype(BF16)


def _q_proj(geo, qan, w_qb, nw256, tab, tq):
    t = qan.shape[0]
    tm = tq
    ncb = geo.t_ctx // tm
    bps = geo.seq_lat // tm
    n = w_qb.shape[1]
    return pl.pallas_call(
        _q_proj_kernel,
        grid=(t // tm,),
        in_specs=[
            pl.BlockSpec((tm, Q_LORA), lambda i: (i, 0)),
            pl.BlockSpec((Q_LORA, n), lambda i: (0, 0)),
            pl.BlockSpec((1, 256), lambda i: (0, 0)),
            pl.BlockSpec((tm, 2 * QK_ROPE), lambda i: (jnp.where(i < ncb, 0, 1 + (i - ncb) % bps), 0)),
        ],
        out_specs=pl.BlockSpec((MLA_HEADS, tm, 256), lambda i: (0, i, 0)),
        out_shape=jax.ShapeDtypeStruct((MLA_HEADS, t, 256), BF16),
        compiler_params=_cparams(("parallel",)),
        name="q_proj",
    )(qan, w_qb, nw256, tab)


def _kv_proj_kernel(a_ref, w_ref, nw_ref, kr_ref, k_ref, v_ref):
    acc = _dot(a_ref[...].astype(BF16), w_ref[...])
    kr = kr_ref[...]
    for h in range(MLA_HEADS):
        kn = _rms(acc[:, h * 256:h * 256 + QK_NOPE], nw_ref[...]).astype(BF16)
        k_ref[h] = jnp.concatenate([kn, kr], axis=-1)
        v_ref[h] = acc[:, h * 256 + QK_NOPE:(h + 1) * 256].astype(BF16)


def _kv_proj(ckv_all, w_kvb, k_norm, krot_all):
    t = ckv_all.shape[0]
    tm = _pick(512, t)
    n = w_kvb.shape[1]
    return pl.pallas_call(
        _kv_proj_kernel,
        grid=(t // tm,),
        in_specs=[
            pl.BlockSpec((tm, KV_LORA), lambda i: (i, 0)),
            pl.BlockSpec((KV_LORA, n), lambda i: (0, 0)),
            pl.BlockSpec((1, QK_NOPE), lambda i: (0, 0)),
            pl.BlockSpec((tm, 2 * QK_ROPE), lambda i: (i, 0)),
        ],
        out_specs=[
            pl.BlockSpec((MLA_HEADS, tm, 256), lambda i: (0, i, 0)),
            pl.BlockSpec((MLA_HEADS, tm, V_HEAD), lambda i: (0, i, 0)),
        ],
        out_shape=[
            jax.ShapeDtypeStruct((MLA_HEADS, t, 256), BF16),
            jax.ShapeDtypeStruct((MLA_HEADS, t, V_HEAD), BF16),
        ],
        compiler_params=_cparams(("parallel",)),
        name="kv_proj",
    )(ckv_all, w_kvb, k_norm.reshape(1, -1), krot_all)


_NT = (((1,), (1,)), ((), ()))
_TN = (((0,), (0,)), ((), ()))
ATTN_SCALE = (QK_NOPE + QK_ROPE) ** -0.5


def _attn_ctx_kernel(q_ref, k_ref, v_ref, o_ref):
    outs = []
    for h in range(MLA_HEADS):
        s = lax.dot_general(q_ref[h], k_ref[h], _NT, preferred_element_type=F32) * ATTN_SCALE
        p = jnp.exp(s - jnp.max(s, axis=-1, keepdims=True))
        l = jnp.sum(p, axis=-1, keepdims=True)
        outs.append(_dot(p.astype(BF16), v_ref[h]) / l)
    o_ref[...] = jnp.concatenate(outs, axis=-1).astype(BF16)


def _attn_lat_kernel(q_ref, kc_ref, vc_ref, kl_ref, vl_ref, o_ref):
    q = q_ref[0]
    s1 = lax.dot_general(q, kc_ref[0], _NT, preferred_element_type=F32) * ATTN_SCALE
    s2 = lax.dot_general(q, kl_ref[0], _NT, preferred_element_type=F32) * ATTN_SCALE
    m = jnp.maximum(jnp.max(s1, axis=-1, keepdims=True), jnp.max(s2, axis=-1, keepdims=True))
    p1 = jnp.exp(s1 - m)
    p2 = jnp.exp(s2 - m)
    l = jnp.sum(p1, axis=-1, keepdims=True) + jnp.sum(p2, axis=-1, keepdims=True)
    o = _dot(p1.astype(BF16), vc_ref[0]) + _dot(p2.astype(BF16), vl_ref[0])
    o_ref[...] = (o / l).astype(BF16)


def _attention(geo, q256, k256, v128):
    t = geo.t
    sc, sl, past = geo.seq_ctx, geo.seq_lat, geo.past
    hv = MLA_HEADS * V_HEAD
    o_ctx = pl.pallas_call(
        _attn_ctx_kernel,
        grid=(geo.n_ctx,),
        in_specs=[
            pl.BlockSpec((MLA_HEADS, sc, 256), lambda b: (0, b, 0)),
            pl.BlockSpec((MLA_HEADS, sc, 256), lambda b: (0, b, 0)),
            pl.BlockSpec((MLA_HEADS, sc, V_HEAD), lambda b: (0, b, 0)),
        ],
        out_specs=pl.BlockSpec((sc, hv), lambda b: (b, 0)),
        out_shape=jax.ShapeDtypeStruct((geo.t_ctx, hv), BF16),
        compiler_params=_cparams(("parallel",)),
        name="attn_ctx",
    )(q256, k256, v128)
    tq = _pick(512, sl)
    nq = sl // tq
    q_off = geo.t_ctx // tq
    kl_off = geo.t_ctx // sl
    kc_off = t // past
    o_lat = pl.pallas_call(
        _attn_lat_kernel,
        grid=(geo.n_lat, MLA_HEADS, nq),
        in_specs=[
            pl.BlockSpec((1, tq, 256), lambda b, h, i: (h, q_off + b * nq + i, 0)),
            pl.BlockSpec((1, past, 256), lambda b, h, i: (h, kc_off + b, 0)),
            pl.BlockSpec((1, past, V_HEAD), lambda b, h, i: (h, kc_off + b, 0)),
            pl.BlockSpec((1, sl, 256), lambda b, h, i: (h, kl_off + b, 0)),
            pl.BlockSpec((1, sl, V_HEAD), lambda b, h, i: (h, kl_off + b, 0)),
        ],
        out_specs=pl.BlockSpec((tq, V_HEAD), lambda b, h, i: (b * nq + i, h)),
        out_shape=jax.ShapeDtypeStruct((geo.t_lat, hv), BF16),
        compiler_params=_cparams(("parallel", "parallel", "arbitrary")),
        name="attn_lat",
    )(q256, k256, v128, k256, v128)
    return jnp.concatenate([o_ctx, o_lat], axis=0)


def _mix_out_kernel(a1_ref, a2_ref, w_ref, x_ref, mod_ref, o_ref, *, kg):
    k1 = a1_ref.shape[1]
    acc = _dot(a1_ref[...], w_ref[:k1, :]) + _dot(a2_ref[...], w_ref[k1:, :])
    o_ref[...] = x_ref[...] + mod_ref[0][kg:kg + 1, :] * acc


def _mix_out(geo, a1, a2, w, x, mod, kg):
    t, d = x.shape
    tm = _pick(512, geo.t_ctx, geo.seq_lat)
    row = geo.mod_row(tm)
    k1, k2 = a1.shape[1], a2.shape[1]
    return pl.pallas_call(
        functools.partial(_mix_out_kernel, kg=kg),
        grid=(t // tm,),
        in_specs=[
            pl.BlockSpec((tm, k1), lambda i: (i, 0)),
            pl.BlockSpec((tm, k2), lambda i: (i, 0)),
            pl.BlockSpec((k1 + k2, d), lambda i: (0, 0)),
            pl.BlockSpec((tm, d), lambda i: (i, 0)),
            pl.BlockSpec((1, 6, d), lambda i: (row(i), 0, 0)),
        ],
        out_specs=pl.BlockSpec((tm, d), lambda i: (i, 0)),
        out_shape=jax.ShapeDtypeStruct((t, d), F32),
        compiler_params=_cparams(("parallel",)),
        name="mix_out",
    )(a1, a2, w, x, mod)


OD_K = GLA_HEADS * GLA_DKH
OD_V = 2 * OD_K
OD_G = OD_V + GLA_HEADS * GLA_DVH
OD_R = OD_G + GLA_HEADS * GLA_DVH
OD_N = OD_R + LANES


def _log_sigmoid(x):
    return jnp.minimum(x, 0.0) - jnp.log1p(jnp.exp(-jnp.abs(x)))


def _gla_chunk(q, k, v, r, wgk, bgk, st, backward):
    c = q.shape[0]
    la = _log_sigmoid(_dot(r.astype(BF16), wgk) + bgk) * (1.0 / GATE_NORM)
    ii = lax.broadcasted_iota(I32, (c, c), 0)
    jj = lax.broadcasted_iota(I32, (c, c), 1)
    causal = (ii <= jj) if backward else (ii >= jj)
    b = jnp.dot(causal.astype(F32), la, precision=HI, preferred_element_type=F32)
    tot = jnp.sum(la, axis=0, keepdims=True)
    q_dec = (q * (GLA_DKH ** -0.5) * jnp.exp(b)).astype(BF16)
    k_dec = (k * jnp.exp(-b)).astype(BF16)
    scores = lax.dot_general(q_dec, k_dec, _NT, preferred_element_type=F32)
    scores = jnp.where(causal, scores, 0.0).astype(BF16)
    vb = v.astype(BF16)
    o = _dot(scores, vb) + lax.dot_general(q_dec, st.astype(BF16), _NT, preferred_element_type=F32)
    k_end = (k * jnp.exp(tot - b)).astype(BF16)
    st_new = jnp.exp(tot) * st + lax.dot_general(vb, k_end, _TN, preferred_element_type=F32)
    return o, st_new


def _gla_kernel(*refs, has_init, want_final):
    qf, kf, vf, rf, qb, kb, vb, rb, wgk_ref, bgk_ref = refs[:10]
    pos = 10
    s0_ref = None
    if has_init:
        s0_ref = refs[pos]
        pos += 1
    of_ref, ob_ref = refs[pos], refs[pos + 1]
    pos += 2
    sfin_ref = None
    if want_final:
        sfin_ref = refs[pos]
        pos += 1
    st_scr = refs[pos]
    c = pl.program_id(2)
    nc = pl.num_programs(2)

    @pl.when(c == 0)
    def _():
        for d in range(2):
            if has_init:
                st_scr[d] = s0_ref[0, d, 0].T
            else:
                st_scr[d] = jnp.zeros(st_scr.shape[1:], F32)

    o_f, st_f = _gla_chunk(qf[...], kf[...], vf[...], rf[...], wgk_ref[0, 0], bgk_ref[0, 0], st_scr[0], False)
    of_ref[...] = o_f
    st_scr[0] = st_f
    o_b, st_b = _gla_chunk(qb[...], kb[...], vb[...], rb[...], wgk_ref[1, 0], bgk_ref[1, 0], st_scr[1], True)
    ob_ref[...] = o_b
    st_scr[1] = st_b

    if want_final:
        @pl.when(c == nc - 1)
        def _():
            sfin_ref[0, 0, 0] = st_scr[0].T
            sfin_ref[0, 1, 0] = st_scr[1].T


def _gla_scan(z, wgk_pad, bgk, n_seq, seq, row0, s0, want_final):
    c = GLA_CHUNK
    nc = seq // c
    base = row0 // c
    fwd = lambda b, h, i: base + b * nc + i
    bwd = lambda b, h, i: base + b * nc + (nc - 1 - i)
    kq, kk, kv_, kr = 0, OD_K // GLA_DKH, OD_V // GLA_DVH, OD_R // LANES

    def zspecs(rowf):
        return [
            pl.BlockSpec((c, GLA_DKH), lambda b, h, i: (rowf(b, h, i), kq + h)),
            pl.BlockSpec((c, GLA_DKH), lambda b, h, i: (rowf(b, h, i), kk + h)),
            pl.BlockSpec((c, GLA_DVH), lambda b, h, i: (rowf(b, h, i), kv_ + h)),
            pl.BlockSpec((c, LANES), lambda b, h, i: (rowf(b, h, i), kr)),
        ]

    in_specs = zspecs(fwd) + zspecs(bwd) + [
        pl.BlockSpec((2, 1, LANES, GLA_DKH), lambda b, h, i: (0, h, 0, 0)),
        pl.BlockSpec((2, 1, 1, GLA_DKH), lambda b, h, i: (0, h, 0, 0)),
    ]
    args = [z] * 8 + [wgk_pad, bgk]
    if s0 is not None:
        in_specs.append(pl.BlockSpec((1, 2, 1, GLA_DKH, GLA_DVH), lambda b, h, i: (b, 0, h, 0, 0)))
        args.append(s0)
    fo = lambda b, h, i: (b * nc + i, h)
    bo = lambda b, h, i: (b * nc + (nc - 1 - i), h)
    out_specs = [pl.BlockSpec((c, GLA_DVH), fo), pl.BlockSpec((c, GLA_DVH), bo)]
    out_shape = [jax.ShapeDtypeStruct((n_seq * seq, GLA_HEADS * GLA_DVH), F32)] * 2
    if want_final:
        out_specs.append(pl.BlockSpec((1, 2, 1, GLA_DKH, GLA_DVH), lambda b, h, i: (b, 0, h, 0, 0)))
        out_shape.append(jax.ShapeDtypeStruct((n_seq, 2, GLA_HEADS, GLA_DKH, GLA_DVH), F32))
    return pl.pallas_call(
        functools.partial(_gla_kernel, has_init=s0 is not None, want_final=want_final),
        grid=(n_seq, GLA_HEADS, nc),
        in_specs=in_specs,
        out_specs=out_specs,
        out_shape=out_shape,
        scratch_shapes=[pltpu.VMEM((2, GLA_DVH, GLA_DKH), F32)],
        compiler_params=_cparams(("parallel", "parallel", "arbitrary")),
        name="gla_scan",
    )(*args)


def _gla_out_kernel(of_ref, ob_ref, g_ref, nw_ref, w_ref, x_ref, mod_ref, o_ref, a_scr, *, kg):
    @pl.when(pl.program_id(1) == 0)
    def _():
        nw = nw_ref[...]
        for h in range(GLA_HEADS):
            sl = slice(h * GLA_DVH, (h + 1) * GLA_DVH)
            o = _rms(of_ref[:, sl] + ob_ref[:, sl], nw)
            g = g_ref[:, sl]
            a_scr[:, sl] = (o * (g * jax.nn.sigmoid(g))).astype(BF16)

    o_ref[...] = x_ref[...] + mod_ref[0][kg:kg + 1, :] * _dot(a_scr[...], w_ref[...])


def _gla_out(geo, o_f, o_b, z, o_norm, w_out, x, mod, kg):
    t, d = x.shape
    dv = GLA_HEADS * GLA_DVH
    tm = _pick(512, geo.t_ctx, geo.seq_lat)
    tn = 512
    row = geo.mod_row(tm)
    return pl.pallas_call(
        functools.partial(_gla_out_kernel, kg=kg),
        grid=(t // tm, d // tn),
        in_specs=[
            pl.BlockSpec((tm, dv), lambda i, j: (i, 0)),
            pl.BlockSpec((tm, dv), lambda i, j: (i, 0)),
            pl.BlockSpec((tm, dv), lambda i, j: (i, OD_G // dv)),
            pl.BlockSpec((1, GLA_DVH), lambda i, j: (0, 0)),
            pl.BlockSpec((dv, tn), lambda i, j: (0, j)),
            pl.BlockSpec((tm, tn), lambda i, j: (i, j)),
            pl.BlockSpec((1, 6, tn), lambda i, j: (row(i), 0, j)),
        ],
        out_specs=pl.BlockSpec((tm, tn), lambda i, j: (i, j)),
        out_shape=jax.ShapeDtypeStruct((t, d), F32),
        scratch_shapes=[pltpu.VMEM((tm, dv), BF16)],
        compiler_params=_cparams(("parallel", "arbitrary")),
        name="gla_out",
    )(o_f, o_b, z, o_norm.reshape(1, -1), w_out, x, mod)


def _pack_bf16_pairs(h):
    n = h.shape[1] // 2
    lo = pltpu.bitcast(h[:, :n].astype(BF16).astype(F32), U32)
    hi = pltpu.bitcast(h[:, n:].astype(BF16).astype(F32), U32)
    return (lo >> 16) | (hi & jnp.uint32(0xFFFF0000))


def _unpack_bf16_pairs(w):
    lo = pltpu.bitcast(w << 16, F32).astype(BF16)
    hi = pltpu.bitcast(w & jnp.uint32(0xFFFF0000), F32).astype(BF16)
    return lo, hi


def _router_kernel(x_ref, nw_ref, mod_ref, wr_ref, br_ref, tri_ref,
                   hp_ref, e_ref, w_ref, rank_ref, cnt_ref, carry_scr, *, ks):
    @pl.when(pl.program_id(0) == 0)
    def _():
        carry_scr[...] = jnp.zeros(carry_scr.shape, F32)

    h = _lnmod(x_ref[...], nw_ref[...], mod_ref[0], ks)
    hp_ref[...] = _pack_bf16_pairs(h)
    lg = lax.dot_general(wr_ref[...], h, _NT, precision=HI, preferred_element_type=F32) + br_ref[...]
    n_e, tm = lg.shape
    rows = lax.broadcasted_iota(I32, (n_e, tm), 0).astype(F32)
    vals, sels, hits = [], [], []
    for _ in range(TOP_K):
        m = jnp.max(lg, axis=0, keepdims=True)
        idx = jnp.min(jnp.where(lg == m, rows, float(n_e)), axis=0, keepdims=True)
        hit = rows == idx
        vals.append(m)
        sels.append(idx)
        hits.append(hit)
        lg = jnp.where(hit, -jnp.inf, lg)
    ex = [jnp.exp(v - vals[0]) for v in vals]
    den = ex[0] + ex[1] + ex[2] + ex[3]
    w_ref[...] = jnp.concatenate([e / den for e in ex], axis=0)
    e_ref[...] = jnp.concatenate(sels, axis=0).astype(I32)
    onehot = jnp.zeros((n_e, tm), F32)
    for hit in hits:
        onehot = jnp.where(hit, 1.0, onehot)
    before = carry_scr[:, 0:1] + _dot(onehot.astype(BF16), tri_ref[...])
    rank_ref[...] = jnp.concatenate(
        [jnp.sum(jnp.where(hit, before, 0.0), axis=0, keepdims=True) for hit in hits], axis=0).astype(I32)
    carry_scr[...] = carry_scr[...] + jnp.sum(onehot, axis=1, keepdims=True)
    cnt_ref[...] = carry_scr[...]


def _router(geo, x, nw, mod, w_router_t, b_router, ks):
    t, d = x.shape
    n_e = w_router_t.shape[0]
    tm = _pick(512, geo.t_ctx, geo.seq_lat)
    row = geo.mod_row(tm)
    tri = (lax.broadcasted_iota(I32, (tm, tm), 0) < lax.broadcasted_iota(I32, (tm, tm), 1)).astype(BF16)
    return pl.pallas_call(
        functools.partial(_router_kernel, ks=ks),
        grid=(t // tm,),
        in_specs=[
            pl.BlockSpec((tm, d), lambda i: (i, 0)),
            pl.BlockSpec((1, d), lambda i: (0, 0)),
            pl.BlockSpec((1, 6, d), lambda i: (row(i), 0, 0)),
            pl.BlockSpec((n_e, d), lambda i: (0, 0)),
            pl.BlockSpec((n_e, 1), lambda i: (0, 0)),
            pl.BlockSpec((tm, tm), lambda i: (0, 0)),
        ],
        out_specs=[
            pl.BlockSpec((tm, d // 2), lambda i: (i, 0)),
            pl.BlockSpec((TOP_K, tm), lambda i: (0, i)),
            pl.BlockSpec((TOP_K, tm), lambda i: (0, i)),
            pl.BlockSpec((TOP_K, tm), lambda i: (0, i)),
            pl.BlockSpec((n_e, LANES), lambda i: (0, 0)),
        ],
        out_shape=[
            jax.ShapeDtypeStruct((t, d // 2), U32),
            jax.ShapeDtypeStruct((TOP_K, t), I32),
            jax.ShapeDtypeStruct((TOP_K, t), F32),
            jax.ShapeDtypeStruct((TOP_K, t), I32),
            jax.ShapeDtypeStruct((n_e, LANES), F32),
        ],
        scratch_shapes=[pltpu.VMEM((n_e, LANES), F32)],
        compiler_params=_cparams(("arbitrary",)),
        name="router",
    )(x, nw.reshape(1, d), mod, w_router_t, b_router.reshape(n_e, 1), tri)


def _row_gather(table, idx, window):
    n = idx.shape[0]
    width = table.shape[1]
    mesh = plsc.VectorSubcoreMesh(core_axis_name="core", subcore_axis_name="subcore")
    idx2 = idx.reshape(n // window, window)

    @functools.partial(
        pl.kernel,
        out_type=jax.ShapeDtypeStruct((n, width), table.dtype),
        mesh=mesh,
        scratch_types=[],
        name="row_gather",
    )
    def gather(t_hbm, i_hbm, o_hbm):
        def body(i_vmem, o_vmem):
            pltpu.sync_copy(t_hbm.at[i_vmem.at[0]], o_vmem)

        pltpu.emit_pipeline(
            body,
            grid=(n // window,),
            in_specs=[pl.BlockSpec((1, window), lambda i: (i, 0))],
            out_specs=[pl.BlockSpec((window, width), lambda i: (i, 0))],
            core_axis_name=("core", "subcore"),
            dimension_semantics=(pltpu.PARALLEL,),
        )(i_hbm, o_hbm)

    return gather(table, idx2)


def _expert_kernel(be_ref, nu_ref, xp_ref, wg_ref, bg_ref, wu_ref, bu_ref, wd_ref, bd_ref, o_ref, x_scr):
    i = pl.program_id(0)
    f = pl.program_id(1)
    half = xp_ref.shape[1]

    @pl.when(i >= nu_ref[0])
    def _():
        @pl.when(f == 0)
        def _():
            o_ref[...] = jnp.zeros(o_ref.shape, F32)

    @pl.when(i < nu_ref[0])
    def _():
        @pl.when(f == 0)
        def _():
            lo, hi = _unpack_bf16_pairs(xp_ref[...])
            x_scr[:, :half] = lo
            x_scr[:, half:] = hi
            o_ref[...] = jnp.broadcast_to(bd_ref[0], o_ref.shape)

        x = x_scr[...]
        gate = jnp.minimum(_dot(x, wg_ref[...].astype(BF16)) + bg_ref[0], SWIGLU_LIMIT)
        up = jnp.clip(_dot(x, wu_ref[...].astype(BF16)) + bu_ref[0], -SWIGLU_LIMIT, SWIGLU_LIMIT)
        hdn = (up + 1.0) * (gate * jax.nn.sigmoid(SWIGLU_ALPHA * gate))
        o_ref[...] += _dot(hdn.astype(BF16), wd_ref[...].astype(BF16))


def _experts(xb, block_e, n_used, layer, w_gate, b_gate, w_up, b_up, w_down, b_down, tm):
    cap, half = xb.shape
    d = 2 * half
    n_e, _, ff = w_gate.shape[1:]
    tf = 256
    nb = cap // tm
    cl = lambda i, nu: jnp.minimum(i, nu[0] - 1)
    wspec_in = pl.BlockSpec((None, None, d, tf), lambda i, f, be, nu: (layer, be[cl(i, nu)], 0, f))
    wspec_out = pl.BlockSpec((None, None, tf, d), lambda i, f, be, nu: (layer, be[cl(i, nu)], f, 0))
    bspec_in = pl.BlockSpec((1, 1, tf), lambda i, f, be, nu: (layer * n_e + be[cl(i, nu)], 0, f))
    bspec_out = pl.BlockSpec((1, 1, d), lambda i, f, be, nu: (layer * n_e + be[cl(i, nu)], 0, 0))
    grid_spec = pltpu.PrefetchScalarGridSpec(
        num_scalar_prefetch=2,
        grid=(nb, ff // tf),
        in_specs=[
            pl.BlockSpec((tm, half), lambda i, f, be, nu: (cl(i, nu), 0)),
            wspec_in, bspec_in, wspec_in, bspec_in, wspec_out, bspec_out,
        ],
        out_specs=pl.BlockSpec((tm, d), lambda i, f, be, nu: (i, 0)),
        scratch_shapes=[pltpu.VMEM((tm, d), BF16)],
    )
    return pl.pallas_call(
        _expert_kernel,
        grid_spec=grid_spec,
        out_shape=jax.ShapeDtypeStruct((cap, d), F32),
        compiler_params=_cparams(("parallel", "arbitrary")),
        name="experts",
    )(block_e, n_used, xb, w_gate, b_gate.reshape(-1, 1, ff), w_up, b_up.reshape(-1, 1, ff),
      w_down, b_down.reshape(-1, 1, d))


def _combine_kernel(y_ref, w_ref, x_ref, mod_ref, o_ref, *, kg):
    w = w_ref[...]
    f = y_ref[0] * w[:, 0:1]
    for k in range(1, TOP_K):
        f = f + y_ref[k] * w[:, k:k + 1]
    o_ref[...] = x_ref[...] + mod_ref[0][kg:kg + 1, :] * f


def _combine(geo, yg, w_tk, x, mod, kg):
    t, d = x.shape
    tm = _pick(256, geo.t_ctx, geo.seq_lat)
    row = geo.mod_row(tm)
    return pl.pallas_call(
        functools.partial(_combine_kernel, kg=kg),
        grid=(t // tm,),
        in_specs=[
            pl.BlockSpec((TOP_K, tm, d), lambda i: (0, i, 0)),
            pl.BlockSpec((tm, TOP_K), lambda i: (i, 0)),
            pl.BlockSpec((tm, d), lambda i: (i, 0)),
            pl.BlockSpec((1, 6, d), lambda i: (row(i), 0, 0)),
        ],
        out_specs=pl.BlockSpec((tm, d), lambda i: (i, 0)),
        out_shape=jax.ShapeDtypeStruct((t, d), F32),
        compiler_params=_cparams(("parallel",)),
        name="combine",
    )(yg, w_tk, x, mod)


MOE_TM = 1024
GATHER_WINDOW = 32


def _moe(geo, x, nw, mod, layer, w_router, b_router, w_gate, b_gate, w_up, b_up, w_down, b_down):
    t, d = x.shape
    n_e = w_router.shape[1]
    hp, top_e, top_w, rank, cnt = _router(geo, x, nw, mod, w_router.T, b_router, 3)
    counts = cnt[:, 0].astype(I32)
    padded = (counts + MOE_TM - 1) // MOE_TM * MOE_TM
    pad_end = jnp.cumsum(padded)
    pad_start = pad_end - padded
    n_assign = t * TOP_K
    nb = -(-n_assign // MOE_TM) + n_e
    cap = nb * MOE_TM
    dest = pad_start[top_e] + rank
    tok = jnp.broadcast_to(jnp.arange(t, dtype=I32)[None, :], (TOP_K, t))
    src = jnp.zeros((cap,), I32).at[dest.reshape(-1)].set(tok.reshape(-1))
    block_e = jnp.minimum(jnp.searchsorted(pad_end, jnp.arange(nb, dtype=I32) * MOE_TM, side="right"),
                          n_e - 1).astype(I32)
    n_used = (pad_end[-1:] // MOE_TM).astype(I32)
    xb = _row_gather(hp, src, GATHER_WINDOW)
    yb = _experts(xb, block_e, n_used, layer, w_gate, b_gate, w_up, b_up, w_down, b_down, MOE_TM)
    yg = _row_gather(yb, dest.reshape(-1), GATHER_WINDOW // 2).reshape(TOP_K, t, d)
    return _combine(geo, yg, top_w.T, x, mod, 5)


_SWAP = np.array([(j + ROPE_FREQ) if (j // ROPE_FREQ) % 2 == 0 else (j - ROPE_FREQ) for j in range(QK_ROPE)])


def _rope_table(seq_lat, tq):
    pos = jnp.arange(seq_lat)
    row = (pos // GRID_W).astype(F32)
    colp = (pos % GRID_W).astype(F32)
    inv = jnp.power(ROPE_THETA, -jnp.arange(ROPE_FREQ, dtype=F32) / ROPE_FREQ)
    ar, ac = row[:, None] * inv, colp[:, None] * inv
    cos = jnp.concatenate([jnp.cos(ar), jnp.cos(ar), jnp.cos(ac), jnp.cos(ac)], axis=1)
    sin = jnp.concatenate([-jnp.sin(ar), jnp.sin(ar), -jnp.sin(ac), jnp.sin(ac)], axis=1)
    ident = jnp.concatenate([jnp.ones((tq, QK_ROPE), F32), jnp.zeros((tq, QK_ROPE), F32)], axis=1)
    return jnp.concatenate([ident, jnp.concatenate([cos, sin], axis=1)], axis=0)


def _even_params(w_in, w_qb, q_norm, q_rope_norm, k_rope_norm):
    d = w_in.shape[0]
    o_kv, o_kr, o_conv = Q_LORA, Q_LORA + KV_LORA, Q_LORA + KV_LORA + QK_ROPE
    kr = w_in[:, o_kr:o_conv]
    w_in_r = jnp.concatenate([w_in[:, o_conv:], w_in[:, :o_kr], kr, kr[:, _SWAP]], axis=1).astype(BF16)
    wq = w_qb.reshape(Q_LORA, MLA_HEADS, QK_NOPE + QK_ROPE)
    wq_r = jnp.concatenate([wq, wq[:, :, QK_NOPE:][:, :, _SWAP]], axis=2).reshape(Q_LORA, MLA_HEADS * 256)
    q_nw = jnp.concatenate([q_norm, q_rope_norm, q_rope_norm[_SWAP]]).reshape(1, 256)
    kr_nw = jnp.concatenate([k_rope_norm, k_rope_norm[_SWAP]]).reshape(1, 2 * QK_ROPE)
    del d
    return w_in_r, wq_r.astype(BF16), q_nw, kr_nw


def _odd_params(w_in, w_gk, b_gk):
    d = w_in.shape[0]
    pad = jnp.zeros((d, OD_N - w_in.shape[1]), w_in.dtype)
    w_in_r = jnp.concatenate([w_in, pad], axis=1).astype(BF16)
    wg = w_gk.reshape(2, GATE_RANK, GLA_HEADS, GLA_DKH).transpose(0, 2, 1, 3)
    wpad = jnp.zeros((2, GLA_HEADS, LANES, GLA_DKH), F32)
    wpad = wpad.at[0, :, :GATE_RANK].set(wg[0]).at[1, :, GATE_RANK:2 * GATE_RANK].set(wg[1])
    return w_in_r, wpad.astype(BF16), b_gk.reshape(2, GLA_HEADS, 1, GLA_DKH)


def kernel(x_prompt, x_sample, cache_mla_ckv, cache_mla_krope, state_gla, c, c_ctx, ada_w, ada_b, norm_mix_w, norm_ffn_w, ev_w_in, ev_q_a_norm, ev_w_qb, ev_kv_a_norm, ev_w_kvb, ev_q_norm, ev_k_norm, ev_q_rope_norm, ev_k_rope_norm, ev_conv_w, ev_w_out, od_w_in, od_w_gk, od_b_gk, od_o_norm, od_w_out, moe_w_router, moe_b_router, moe_w_gate, moe_b_gate, moe_w_up, moe_b_up, moe_w_down, moe_b_down):
    geo = _Geo(x_prompt, x_sample, cache_mla_ckv)
    d = geo.d
    depth = ada_w.shape[0]
    x = jnp.concatenate([x_prompt.reshape(geo.t_ctx, d), x_sample.reshape(geo.t_lat, d)], axis=0)
    cond = jnp.concatenate([c_ctx[None, :], c, jnp.zeros((16 - 1 - geo.n_lat, d), F32)], axis=0)
    mods = _adaln(cond, ada_w, ada_b).reshape(depth, 16, 6, d)
    tq = _pick(512, geo.t_ctx, geo.seq_lat)
    tab = _rope_table(geo.seq_lat, tq)
    new_ckv, new_krope, new_gla = [], [], []
    for l in range(depth):
        i = l // 2
        mod = mods[l]
        if l % 2 == 0:
            w_in_r, wq_r, q_nw, kr_nw = _even_params(ev_w_in[i], ev_w_qb[i], ev_q_norm[i], ev_q_rope_norm[i],
                                                     ev_k_rope_norm[i])
            z = _lnmod_mm(geo, x, norm_mix_w[l], mod, w_in_r, 0)
            conv, qan, ckv, krn, krot = _even_mid(geo, z, ev_conv_w[i], ev_q_a_norm[i], ev_kv_a_norm[i], kr_nw,
                                                  tab, tq)
            q256 = _q_proj(geo, qan, wq_r, q_nw, tab, tq)
            kc = cache_mla_krope[:, i].reshape(geo.n_lat * geo.past, QK_ROPE)
            ckv_all = jnp.concatenate([ckv, cache_mla_ckv[:, i].reshape(geo.n_lat * geo.past, KV_LORA)], axis=0)
            krot_all = jnp.concatenate([krot, jnp.concatenate([kc, kc], axis=1).astype(BF16)], axis=0)
            k256, v128 = _kv_proj(ckv_all, ev_w_kvb[i].astype(BF16), ev_k_norm[i], krot_all)
            o = _attention(geo, q256, k256, v128)
            x = _mix_out(geo, o, conv, ev_w_out[i].astype(BF16), x, mod, 2)
            new_ckv.append(ckv[:geo.t_ctx].reshape(geo.n_ctx, geo.seq_ctx, KV_LORA))
            new_krope.append(krn[:geo.t_ctx].reshape(geo.n_ctx, geo.seq_ctx, QK_ROPE))
        else:
            w_in_r, wgk_pad, bgk = _odd_params(od_w_in[i], od_w_gk[i], od_b_gk[i])
            z = _lnmod_mm(geo, x, norm_mix_w[l], mod, w_in_r, 0)
            of_c, ob_c, s_fin = _gla_scan(z, wgk_pad, bgk, geo.n_ctx, geo.seq_ctx, 0, None, True)
            of_l, ob_l = _gla_scan(z, wgk_pad, bgk, geo.n_lat, geo.seq_lat, geo.t_ctx, state_gla[:, i], False)
            o_f = jnp.concatenate([of_c, of_l], axis=0)
            o_b = jnp.concatenate([ob_c, ob_l], axis=0)
            x = _gla_out(geo, o_f, o_b, z, od_o_norm[i], od_w_out[i].astype(BF16), x, mod, 2)
            new_gla.append(s_fin)
        x = _moe(geo, x, norm_ffn_w[l], mod, l, moe_w_router[l], moe_b_router[l], moe_w_gate, moe_b_gate,
                 moe_w_up, moe_b_up, moe_w_down, moe_b_down)
    xp = x[:geo.t_ctx].reshape(x_prompt.shape)
    xs = x[geo.t_ctx:].reshape(x_sample.shape)
    return (xp, xs, jnp.stack(new_ckv, axis=1), jnp.stack(new_krope, axis=1), jnp.stack(new_gla, axis=1))
```

```python
import functools

import jax
import jax.numpy as jnp
import numpy as np
from jax import lax
from jax.experimental import pallas as pl
from jax.experimental.pallas import tpu as pltpu

F32 = jnp.float32
BF16 = jnp.bfloat16
I32 = jnp.int32
U32 = jnp.uint32

EPS = 1e-6
GRID_W = 64
MLA_HEADS = 8
QK_NOPE = 128
QK_ROPE = 64
V_HEAD = 128
Q_LORA = 512
KV_LORA = 256
ROPE_THETA = 10000.0
ROPE_FREQ = QK_ROPE // 4
CONV_W = 1024
GLA_HEADS = 4
GLA_DKH = 256
GLA_DVH = 512
GATE_RANK = 16
GATE_NORM = 16.0
GLA_CHUNK = 64
TOP_K = 4
SWIGLU_ALPHA = 1.702
SWIGLU_LIMIT = 7.0

V7X_VMEM_LIMIT = 56 * 1024 * 1024
LANES = 128
HI = lax.Precision.HIGHEST


def _cparams(sem):
    return pltpu.CompilerParams(dimension_semantics=sem, vmem_limit_bytes=V7X_VMEM_LIMIT)


def _pick(want, *dims):
    t = want
    while any(d % t for d in dims):
        t //= 2
    return t


def _dot(a, b):
    return jnp.dot(a, b, preferred_element_type=F32)


def _rms(x, w):
    return x * lax.rsqrt(jnp.mean(x * x, axis=-1, keepdims=True) + EPS) * w


def _lnmod(x, nw, mod, ks):
    return _rms(x, nw) * (1.0 + mod[ks + 1:ks + 2, :]) + mod[ks:ks + 1, :]


class _Geo:
    def __init__(self, x_prompt, x_sample, cache_ckv):
        self.n_ctx, self.seq_ctx, self.d = x_prompt.shape
        self.n_lat, self.seq_lat, _ = x_sample.shape
        self.past = cache_ckv.shape[2]
        self.t_ctx = self.n_ctx * self.seq_ctx
        self.t_lat = self.n_lat * self.seq_lat
        self.t = self.t_ctx + self.t_lat

    def mod_row(self, tm):
        t_ctx, seq_lat = self.t_ctx, self.seq_lat
        return lambda i: jnp.where(i * tm < t_ctx, 0, 1 + (i * tm - t_ctx) // seq_lat)


def _adaln_kernel(c_ref, w_ref, b_ref, o_ref):
    c = c_ref[...]
    a = (c * jax.nn.sigmoid(c)).astype(BF16)
    o_ref[0] = _dot(a, w_ref[0].astype(BF16)) + b_ref[0]


def _adaln(cond16, ada_w, ada_b):
    depth, d, n = ada_w.shape
    tn = 1024
    return pl.pallas_call(
        _adaln_kernel,
        grid=(depth, n // tn),
        in_specs=[
            pl.BlockSpec((16, d), lambda l, j: (0, 0)),
            pl.BlockSpec((1, d, tn), lambda l, j: (l, 0, j)),
            pl.BlockSpec((1, 1, tn), lambda l, j: (l, 0, j)),
        ],
        out_specs=pl.BlockSpec((1, 16, tn), lambda l, j: (l, 0, j)),
        out_shape=jax.ShapeDtypeStruct((depth, 16, n), F32),
        compiler_params=_cparams(("parallel", "parallel")),
        name="adaln",
    )(cond16, ada_w, ada_b.reshape(depth, 1, n))


def _lnmod_mm_kernel(x_ref, nw_ref, mod_ref, w_ref, o_ref, h_scr, *, ks):
    @pl.when(pl.program_id(1) == 0)
    def _():
        h_scr[...] = _lnmod(x_ref[...], nw_ref[...], mod_ref[0], ks).astype(BF16)

    o_ref[...] = _dot(h_scr[...], w_ref[...])


def _lnmod_mm(geo, x, nw, mod, w, ks):
    t, d = x.shape
    n = w.shape[1]
    tm = _pick(1024, geo.t_ctx, geo.seq_lat)
    tn = 512
    row = geo.mod_row(tm)
    return pl.pallas_call(
        functools.partial(_lnmod_mm_kernel, ks=ks),
        grid=(t // tm, pl.cdiv(n, tn)),
        in_specs=[
            pl.BlockSpec((tm, d), lambda i, j: (i, 0)),
            pl.BlockSpec((1, d), lambda i, j: (0, 0)),
            pl.BlockSpec((1, 6, d), lambda i, j: (row(i), 0, 0)),
            pl.BlockSpec((d, tn), lambda i, j: (0, j)),
        ],
        out_specs=pl.BlockSpec((tm, tn), lambda i, j: (i, j)),
        out_shape=jax.ShapeDtypeStruct((t, n), F32),
        scratch_shapes=[pltpu.VMEM((tm, d), BF16)],
        compiler_params=_cparams(("parallel", "arbitrary")),
        name="lnmod_mm",
    )(x, nw.reshape(1, d), mod, w)


EV_QA = 3 * CONV_W
EV_KVA = EV_QA + Q_LORA
EV_KR = EV_KVA + KV_LORA
EV_N = EV_KR + 2 * QK_ROPE


def _even_mid_kernel(bg_ref, cg_ref, xv_ref, cgp_ref, xvp_ref, cgn_ref, xvn_ref, qa_ref, kva_ref, kr_ref,
                     cw_ref, qaw_ref, kvw_ref, krw_ref, tab_ref,
                     conv_ref, qan_ref, ckv_ref, krn_ref, krot_ref, *, t_ctx, seq_ctx, seq_lat):
    i = pl.program_id(0)
    tm = bg_ref.shape[0]
    u = cg_ref[...] * xv_ref[...]
    rows = lax.broadcasted_iota(I32, (tm, 1), 0)
    tok = i * tm + rows
    pos = jnp.where(tok < t_ctx, tok & (seq_ctx - 1), (tok - t_ctx) & (seq_lat - 1))
    last = jnp.where(tok < t_ctx, seq_ctx - 1, seq_lat - 1)
    u_m1 = jnp.where(rows == 0, cgp_ref[7:8, :] * xvp_ref[7:8, :], pltpu.roll(u, 1, axis=0))
    u_p1 = jnp.where(rows == tm - 1, cgn_ref[0:1, :] * xvn_ref[0:1, :], pltpu.roll(u, tm - 1, axis=0))
    u_m1 = jnp.where(pos == 0, 0.0, u_m1)
    u_p1 = jnp.where(pos == last, 0.0, u_p1)
    cw = cw_ref[...]
    conv = bg_ref[...] * (u_m1 * cw[0:1, :] + u * cw[1:2, :] + u_p1 * cw[2:3, :])
    conv_ref[...] = conv.astype(BF16)
    qan_ref[...] = _rms(qa_ref[...], qaw_ref[...]).astype(BF16)
    ckv_ref[...] = _rms(kva_ref[...], kvw_ref[...])
    kr = kr_ref[...]
    krn = _rms(kr, krw_ref[...])
    krn_ref[...] = krn[:, :QK_ROPE]
    y = krn * tab_ref[...]
    krot_ref[...] = (y + pltpu.roll(y, QK_ROPE, axis=1)).astype(BF16)


def _even_mid(geo, z, conv_w, qa_w, kv_w, kr_w128, tab, tq):
    t = z.shape[0]
    tm = tq
    ncb = geo.t_ctx // tm
    bps = geo.seq_lat // tm
    nb8 = t // 8
    r8 = tm // 8
    assert geo.seq_ctx & (geo.seq_ctx - 1) == 0 and geo.seq_lat & (geo.seq_lat - 1) == 0
    tab_idx = lambda i: jnp.where(i < ncb, 0, 1 + (i - ncb) % bps)
    col = lambda c: (lambda i: (i, c))
    in_specs = [
        pl.BlockSpec((tm, CONV_W), col(0)),
        pl.BlockSpec((tm, CONV_W), col(1)),
        pl.BlockSpec((tm, CONV_W), col(2)),
        pl.BlockSpec((8, CONV_W), lambda i: (jnp.maximum(i * r8 - 1, 0), 1)),
        pl.BlockSpec((8, CONV_W), lambda i: (jnp.maximum(i * r8 - 1, 0), 2)),
        pl.BlockSpec((8, CONV_W), lambda i: (jnp.minimum((i + 1) * r8, nb8 - 1), 1)),
        pl.BlockSpec((8, CONV_W), lambda i: (jnp.minimum((i + 1) * r8, nb8 - 1), 2)),
        pl.BlockSpec((tm, Q_LORA), col(EV_QA // Q_LORA)),
        pl.BlockSpec((tm, KV_LORA), col(EV_KVA // KV_LORA)),
        pl.BlockSpec((tm, 2 * QK_ROPE), col(EV_KR // (2 * QK_ROPE))),
        pl.BlockSpec((3, CONV_W), lambda i: (0, 0)),
        pl.BlockSpec((1, Q_LORA), lambda i: (0, 0)),
        pl.BlockSpec((1, KV_LORA), lambda i: (0, 0)),
        pl.BlockSpec((1, 2 * QK_ROPE), lambda i: (0, 0)),
        pl.BlockSpec((tm, 2 * QK_ROPE), lambda i: (tab_idx(i), 0)),
    ]
    out_specs = [
        pl.BlockSpec((tm, CONV_W), lambda i: (i, 0)),
        pl.BlockSpec((tm, Q_LORA), lambda i: (i, 0)),
        pl.BlockSpec((tm, KV_LORA), lambda i: (i, 0)),
        pl.BlockSpec((tm, QK_ROPE), lambda i: (i, 0)),
        pl.BlockSpec((tm, 2 * QK_ROPE), lambda i: (i, 0)),
    ]
    out_shape = [
        jax.ShapeDtypeStruct((t, CONV_W), BF16),
        jax.ShapeDtypeStruct((t, Q_LORA), BF16),
        jax.ShapeDtypeStruct((t, KV_LORA), F32),
        jax.ShapeDtypeStruct((t, QK_ROPE), F32),
        jax.ShapeDtypeStruct((t, 2 * QK_ROPE), BF16),
    ]
    return pl.pallas_call(
        functools.partial(_even_mid_kernel, t_ctx=geo.t_ctx, seq_ctx=geo.seq_ctx, seq_lat=geo.seq_lat),
        grid=(t // tm,),
        in_specs=in_specs,
        out_specs=out_specs,
        out_shape=out_shape,
        compiler_params=_cparams(("parallel",)),
        name="even_mid",
    )(z, z, z, z, z, z, z, z, z, z, conv_w, qa_w.reshape(1, -1), kv_w.reshape(1, -1), kr_w128, tab)


def _q_proj_kernel(a_ref, w_ref, nw_ref, tab_ref, o_ref):
    acc = _dot(a_ref[...], w_ref[...])
    nw = nw_ref[...]
    tab = tab_ref[...]
    for h in range(MLA_HEADS):
        n = acc[:, h * 256:h * 256 + QK_NOPE]
        r = acc[:, h * 256 + QK_NOPE:(h + 1) * 256]
        qn = _rms(n, nw[:, :QK_NOPE])
        y = _rms(r, nw[:, QK_NOPE:]) * tab
        o_ref[h] = (jnp.concatenate([qn, y], axis=-1) * Q_PRESCALE).astype(BF16)


def _q_proj(geo, qan, w_qb, nw256, tab, tq):
    t = qan.shape[0]
    tm = tq
    ncb = geo.t_ctx // tm
    bps = geo.seq_lat // tm
    n = w_qb.shape[1]
    return pl.pallas_call(
        _q_proj_kernel,
        grid=(t // tm,),
        in_specs=[
            pl.BlockSpec((tm, Q_LORA), lambda i: (i, 0)),
            pl.BlockSpec((Q_LORA, n), lambda i: (0, 0)),
            pl.BlockSpec((1, 256), lambda i: (0, 0)),
            pl.BlockSpec((tm, 2 * QK_ROPE), lambda i: (jnp.where(i < ncb, 0, 1 + (i - ncb) % bps), 0)),
        ],
        out_specs=pl.BlockSpec((MLA_HEADS, tm, 256), lambda i: (0, i, 0)),
        out_shape=jax.ShapeDtypeStruct((MLA_HEADS, t, 256), BF16),
        compiler_params=_cparams(("parallel",)),
        name="q_proj",
    )(qan, w_qb, nw256, tab)


def _kv_proj_kernel(a_ref, w_ref, nw_ref, kr_ref, k_ref, v_ref):
    acc = _dot(a_ref[...].astype(BF16), w_ref[...])
    kr = kr_ref[...]
    ones = jnp.ones((acc.shape[0], V_HEAD), BF16)
    for h in range(MLA_HEADS):
        kn = _rms(acc[:, h * 256:h * 256 + QK_NOPE], nw_ref[...]).astype(BF16)
        k_ref[h] = jnp.concatenate([kn, kr], axis=-1)
        v_ref[h] = jnp.concatenate([acc[:, h * 256 + QK_NOPE:(h + 1) * 256].astype(BF16), ones], axis=-1)


def _kv_proj(ckv_all, w_kvb, k_norm, krot_all):
    t = ckv_all.shape[0]
    tm = _pick(512, t)
    n = w_kvb.shape[1]
    return pl.pallas_call(
        _kv_proj_kernel,
        grid=(t // tm,),
        in_specs=[
            pl.BlockSpec((tm, KV_LORA), lambda i: (i, 0)),
            pl.BlockSpec((KV_LORA, n), lambda i: (0, 0)),
            pl.BlockSpec((1, QK_NOPE), lambda i: (0, 0)),
            pl.BlockSpec((tm, 2 * QK_ROPE), lambda i: (i, 0)),
        ],
        out_specs=[
            pl.BlockSpec((MLA_HEADS, tm, 256), lambda i: (0, i, 0)),
            pl.BlockSpec((MLA_HEADS, tm, 2 * V_HEAD), lambda i: (0, i, 0)),
        ],
        out_shape=[
            jax.ShapeDtypeStruct((MLA_HEADS, t, 256), BF16),
            jax.ShapeDtypeStruct((MLA_HEADS, t, 2 * V_HEAD), BF16),
        ],
        compiler_params=_cparams(("parallel",)),
        name="kv_proj",
    )(ckv_all, w_kvb, k_norm.reshape(1, -1), krot_all)


_NT = (((1,), (1,)), ((), ()))
_TN = (((0,), (0,)), ((), ()))
Q_PRESCALE = (QK_NOPE + QK_ROPE) ** -0.5 * 1.4426950408889634
ATTN_KEY_CHUNK = 512


def _softmax_step(q, k, v, m, acc):
    s = lax.dot_general(q, k, _NT, preferred_element_type=F32)
    m_new = jnp.maximum(m, jnp.max(s, axis=-1, keepdims=True))
    p = jnp.exp2(s - m_new)
    acc = acc * jnp.exp2(m - m_new) + _dot(p.astype(BF16), v)
    return m_new, acc


def _attn_ctx_kernel(q_ref, k_ref, v_ref, prev_ref, o_ref):
    del prev_ref
    outs = []
    for h in range(MLA_HEADS):
        s = lax.dot_general(q_ref[h], k_ref[h], _NT, preferred_element_type=F32)
        p = jnp.exp2(s - jnp.max(s, axis=-1, keepdims=True))
        o = _dot(p.astype(BF16), v_ref[h])
        outs.append(o[:, :V_HEAD] / o[:, V_HEAD:V_HEAD + 1])
    o_ref[...] = jnp.concatenate(outs, axis=-1).astype(BF16)


def _attn_lat_kernel(q_ref, kc_ref, vc_ref, kl_ref, vl_ref, prev_ref, o_ref):
    del prev_ref
    q = q_ref[0]
    tq = q.shape[0]
    m = jnp.full((tq, 1), -jnp.inf, F32)
    acc = jnp.zeros((tq, 2 * V_HEAD), F32)
    for k_ref, v_ref in ((kc_ref, vc_ref), (kl_ref, vl_ref)):
        n = k_ref.shape[1]
        ck = min(ATTN_KEY_CHUNK, n)
        for j in range(n // ck):
            m, acc = _softmax_step(q, k_ref[0, j * ck:(j + 1) * ck, :], v_ref[0, j * ck:(j + 1) * ck, :], m, acc)
    o_ref[...] = (acc[:, :V_HEAD] / acc[:, V_HEAD:V_HEAD + 1]).astype(BF16)


def _attention(geo, q256, k256, v256, o_buf):
    t = geo.t
    sc, sl, past = geo.seq_ctx, geo.seq_lat, geo.past
    hv = MLA_HEADS * V_HEAD
    vw = 2 * V_HEAD
    o_ctx = pl.pallas_call(
        _attn_ctx_kernel,
        grid=(geo.n_ctx,),
        in_specs=[
            pl.BlockSpec((MLA_HEADS, sc, 256), lambda b: (0, b, 0)),
            pl.BlockSpec((MLA_HEADS, sc, 256), lambda b: (0, b, 0)),
            pl.BlockSpec((MLA_HEADS, sc, vw), lambda b: (0, b, 0)),
            pl.BlockSpec(memory_space=pl.ANY),
        ],
        out_specs=pl.BlockSpec((sc, hv), lambda b: (b, 0)),
        out_shape=jax.ShapeDtypeStruct((t, hv), BF16),
        input_output_aliases={3: 0},
        compiler_params=_cparams(("parallel",)),
        name="attn_ctx",
    )(q256, k256, v256, o_buf)
    tq = _pick(512, sl)
    nq = sl // tq
    q_off = geo.t_ctx // tq
    kl_off = geo.t_ctx // sl
    kc_off = t // past
    return pl.pallas_call(
        _attn_lat_kernel,
        grid=(geo.n_lat, MLA_HEADS, nq),
        in_specs=[
            pl.BlockSpec((1, tq, 256), lambda b, h, i: (h, q_off + b * nq + i, 0)),
            pl.BlockSpec((1, past, 256), lambda b, h, i: (h, kc_off + b, 0)),
            pl.BlockSpec((1, past, vw), lambda b, h, i: (h, kc_off + b, 0)),
            pl.BlockSpec((1, sl, 256), lambda b, h, i: (h, kl_off + b, 0)),
            pl.BlockSpec((1, sl, vw), lambda b, h, i: (h, kl_off + b, 0)),
            pl.BlockSpec(memory_space=pl.ANY),
        ],
        out_specs=pl.BlockSpec((tq, V_HEAD), lambda b, h, i: (q_off + b * nq + i, h)),
        out_shape=jax.ShapeDtypeStruct((t, hv), BF16),
        input_output_aliases={5: 0},
        compiler_params=_cparams(("parallel", "parallel", "arbitrary")),
        name="attn_lat",
    )(q256, k256, v256, k256, v256, o_ctx)


def _mix_out_kernel(a1_ref, a2_ref, w_ref, x_ref, mod_ref, o_ref, *, kg):
    k1 = a1_ref.shape[1]
    acc = _dot(a1_ref[...], w_ref[:k1, :]) + _dot(a2_ref[...], w_ref[k1:, :])
    o_ref[...] = x_ref[...] + mod_ref[0][kg:kg + 1, :] * acc


def _mix_out(geo, a1, a2, w, x, mod, kg):
    t, d = x.shape
    tm = _pick(512, geo.t_ctx, geo.seq_lat)
    row = geo.mod_row(tm)
    k1, k2 = a1.shape[1], a2.shape[1]
    return pl.pallas_call(
        functools.partial(_mix_out_kernel, kg=kg),
        grid=(t // tm,),
        in_specs=[
            pl.BlockSpec((tm, k1), lambda i: (i, 0)),
            pl.BlockSpec((tm, k2), lambda i: (i, 0)),
            pl.BlockSpec((k1 + k2, d), lambda i: (0, 0)),
            pl.BlockSpec((tm, d), lambda i: (i, 0)),
            pl.BlockSpec((1, 6, d), lambda i: (row(i), 0, 0)),
        ],
        out_specs=pl.BlockSpec((tm, d), lambda i: (i, 0)),
        out_shape=jax.ShapeDtypeStruct((t, d), F32),
        compiler_params=_cparams(("parallel",)),
        name="mix_out",
    )(a1, a2, w, x, mod)


OD_K = GLA_HEADS * GLA_DKH
OD_V = 2 * OD_K
OD_G = OD_V + GLA_HEADS * GLA_DVH
OD_R = OD_G + GLA_HEADS * GLA_DVH
OD_N = OD_R + LANES


def _log_sigmoid(x):
    return jnp.minimum(x, 0.0) - jnp.log1p(jnp.exp(-jnp.abs(x)))


def _gla_chunk(q, k, v, r, wgk, bgk, st, backward):
    c = q.shape[0]
    la = _log_sigmoid(_dot(r.astype(BF16), wgk) + bgk) * (1.0 / GATE_NORM)
    ii = lax.broadcasted_iota(I32, (c, c), 0)
    jj = lax.broadcasted_iota(I32, (c, c), 1)
    causal = (ii <= jj) if backward else (ii >= jj)
    b = jnp.dot(causal.astype(F32), la, precision=HI, preferred_element_type=F32)
    tot = jnp.sum(la, axis=0, keepdims=True)
    q_dec = (q * (GLA_DKH ** -0.5) * jnp.exp(b)).astype(BF16)
    k_dec = (k * jnp.exp(-b)).astype(BF16)
    scores = lax.dot_general(q_dec, k_dec, _NT, preferred_element_type=F32)
    scores = jnp.where(causal, scores, 0.0).astype(BF16)
    vb = v.astype(BF16)
    o = _dot(scores, vb) + lax.dot_general(q_dec, st.astype(BF16), _NT, preferred_element_type=F32)
    k_end = (k * jnp.exp(tot - b)).astype(BF16)
    st_new = jnp.exp(tot) * st + lax.dot_general(vb, k_end, _TN, preferred_element_type=F32)
    return o, st_new


def _gla_kernel(*refs, has_init, want_final):
    qf, kf, vf, rf, qb, kb, vb, rb, wgk_ref, bgk_ref = refs[:10]
    pos = 10
    s0_ref = None
    if has_init:
        s0_ref = refs[pos]
        pos += 1
    pos += 2
    of_ref, ob_ref = refs[pos], refs[pos + 1]
    pos += 2
    sfin_ref = None
    if want_final:
        sfin_ref = refs[pos]
        pos += 1
    st_scr = refs[pos]
    c = pl.program_id(1)
    nc = pl.num_programs(1)

    @pl.when(c == 0)
    def _():
        for d in range(2):
            for h in range(GLA_HEADS):
                if has_init:
                    st_scr[d, h] = s0_ref[0, d, h].T
                else:
                    st_scr[d, h] = jnp.zeros(st_scr.shape[2:], F32)

    for h in range(GLA_HEADS):
        ks = slice(h * GLA_DKH, (h + 1) * GLA_DKH)
        vs = slice(h * GLA_DVH, (h + 1) * GLA_DVH)
        o_f, st_f = _gla_chunk(qf[:, ks], kf[:, ks], vf[:, vs], rf[...], wgk_ref[0, h], bgk_ref[0, h],
                               st_scr[0, h], False)
        of_ref[:, vs] = o_f
        st_scr[0, h] = st_f
        o_b, st_b = _gla_chunk(qb[:, ks], kb[:, ks], vb[:, vs], rb[...], wgk_ref[1, h], bgk_ref[1, h],
                               st_scr[1, h], True)
        ob_ref[:, vs] = o_b
        st_scr[1, h] = st_b

    if want_final:
        @pl.when(c == nc - 1)
        def _():
            for d in range(2):
                for h in range(GLA_HEADS):
                    sfin_ref[0, d, h] = st_scr[d, h].T


def _gla_scan(z, wgk_pad, bgk, n_seq, seq, row0, s0, want_final, bufs):
    t = z.shape[0]
    c = GLA_CHUNK
    nc = seq // c
    base = row0 // c
    fwd = lambda b, i: base + b * nc + i
    bwd = lambda b, i: base + b * nc + (nc - 1 - i)
    dk, dv = GLA_HEADS * GLA_DKH, GLA_HEADS * GLA_DVH

    def zspecs(rowf):
        return [
            pl.BlockSpec((c, dk), lambda b, i: (rowf(b, i), 0)),
            pl.BlockSpec((c, dk), lambda b, i: (rowf(b, i), OD_K // dk)),
            pl.BlockSpec((c, dv), lambda b, i: (rowf(b, i), OD_V // dv)),
            pl.BlockSpec((c, LANES), lambda b, i: (rowf(b, i), OD_R // LANES)),
        ]

    in_specs = zspecs(fwd) + zspecs(bwd) + [
        pl.BlockSpec((2, GLA_HEADS, LANES, GLA_DKH), lambda b, i: (0, 0, 0, 0)),
        pl.BlockSpec((2, GLA_HEADS, 1, GLA_DKH), lambda b, i: (0, 0, 0, 0)),
    ]
    args = [z] * 8 + [wgk_pad, bgk]
    if s0 is not None:
        in_specs.append(pl.BlockSpec((1, 2, GLA_HEADS, GLA_DKH, GLA_DVH), lambda b, i: (b, 0, 0, 0, 0)))
        args.append(s0)
    aliases = {len(args): 0, len(args) + 1: 1}
    in_specs += [pl.BlockSpec(memory_space=pl.ANY)] * 2
    args += list(bufs)
    out_specs = [pl.BlockSpec((c, dv), lambda b, i: (fwd(b, i), 0)), pl.BlockSpec((c, dv), lambda b, i: (bwd(b, i), 0))]
    out_shape = [jax.ShapeDtypeStruct((t, dv), F32)] * 2
    if want_final:
        out_specs.append(pl.BlockSpec((1, 2, GLA_HEADS, GLA_DKH, GLA_DVH), lambda b, i: (b, 0, 0, 0, 0)))
        out_shape.append(jax.ShapeDtypeStruct((n_seq, 2, GLA_HEADS, GLA_DKH, GLA_DVH), F32))
    return pl.pallas_call(
        functools.partial(_gla_kernel, has_init=s0 is not None, want_final=want_final),
        grid=(n_seq, nc),
        in_specs=in_specs,
        out_specs=out_specs,
        out_shape=out_shape,
        input_output_aliases=aliases,
        scratch_shapes=[pltpu.VMEM((2, GLA_HEADS, GLA_DVH, GLA_DKH), F32)],
        compiler_params=_cparams(("parallel", "arbitrary")),
        name="gla_scan",
    )(*args)


def _gla_out_kernel(of_ref, ob_ref, g_ref, nw_ref, w_ref, x_ref, mod_ref, o_ref, a_scr, *, kg):
    @pl.when(pl.program_id(1) == 0)
    def _():
        nw = nw_ref[...]
        for h in range(GLA_HEADS):
            sl = slice(h * GLA_DVH, (h + 1) * GLA_DVH)
            o = _rms(of_ref[:, sl] + ob_ref[:, sl], nw)
            g = g_ref[:, sl]
            a_scr[:, sl] = (o * (g * jax.nn.sigmoid(g))).astype(BF16)

    o_ref[...] = x_ref[...] + mod_ref[0][kg:kg + 1, :] * _dot(a_scr[...], w_ref[...])


def _gla_out(geo, o_f, o_b, z, o_norm, w_out, x, mod, kg):
    t, d = x.shape
    dv = GLA_HEADS * GLA_DVH
    tm = _pick(512, geo.t_ctx, geo.seq_lat)
    tn = 512
    row = geo.mod_row(tm)
    return pl.pallas_call(
        functools.partial(_gla_out_kernel, kg=kg),
        grid=(t // tm, d // tn),
        in_specs=[
            pl.BlockSpec((tm, dv), lambda i, j: (i, 0)),
            pl.BlockSpec((tm, dv), lambda i, j: (i, 0)),
            pl.BlockSpec((tm, dv), lambda i, j: (i, OD_G // dv)),
            pl.BlockSpec((1, GLA_DVH), lambda i, j: (0, 0)),
            pl.BlockSpec((dv, tn), lambda i, j: (0, j)),
            pl.BlockSpec((tm, tn), lambda i, j: (i, j)),
            pl.BlockSpec((1, 6, tn), lambda i, j: (row(i), 0, j)),
        ],
        out_specs=pl.BlockSpec((tm, tn), lambda i, j: (i, j)),
        out_shape=jax.ShapeDtypeStruct((t, d), F32),
        scratch_shapes=[pltpu.VMEM((tm, dv), BF16)],
        compiler_params=_cparams(("parallel", "arbitrary")),
        name="gla_out",
    )(o_f, o_b, z, o_norm.reshape(1, -1), w_out, x, mod)


def _pack_bf16_pairs(h):
    n = h.shape[1] // 2
    lo = pltpu.bitcast(h[:, :n].astype(BF16).astype(F32), U32)
    hi = pltpu.bitcast(h[:, n:].astype(BF16).astype(F32), U32)
    return (lo >> 16) | (hi & jnp.uint32(0xFFFF0000))


def _unpack_bf16_pairs(w):
    lo = pltpu.bitcast(w << 16, F32).astype(BF16)
    hi = pltpu.bitcast(w & jnp.uint32(0xFFFF0000), F32).astype(BF16)
    return lo, hi


def _router_kernel(x_ref, nw_ref, mod_ref, wr_ref, br_ref, tri_ref,
                   hp_ref, e_ref, w_ref, rank_ref, cnt_ref, carry_scr, *, ks):
    @pl.when(pl.program_id(0) == 0)
    def _():
        carry_scr[...] = jnp.zeros(carry_scr.shape, F32)

    h = _lnmod(x_ref[...], nw_ref[...], mod_ref[0], ks)
    hp_ref[...] = _pack_bf16_pairs(h)
    lg = lax.dot_general(wr_ref[...], h, _NT, precision=HI, preferred_element_type=F32) + br_ref[...]
    n_e, tm = lg.shape
    rows = lax.broadcasted_iota(I32, (n_e, tm), 0).astype(F32)
    vals, sels, hits = [], [], []
    for _ in range(TOP_K):
        m = jnp.max(lg, axis=0, keepdims=True)
        idx = jnp.min(jnp.where(lg == m, rows, float(n_e)), axis=0, keepdims=True)
        hit = rows == idx
        vals.append(m)
        sels.append(idx)
        hits.append(hit)
        lg = jnp.where(hit, -jnp.inf, lg)
    ex = [jnp.exp(v - vals[0]) for v in vals]
    den = ex[0] + ex[1] + ex[2] + ex[3]
    w_ref[...] = jnp.concatenate([e / den for e in ex], axis=0)
    e_ref[...] = jnp.concatenate(sels, axis=0).astype(I32)
    onehot = jnp.zeros((n_e, tm), F32)
    for hit in hits:
        onehot = jnp.where(hit, 1.0, onehot)
    before = carry_scr[:, 0:1] + _dot(onehot.astype(BF16), tri_ref[...])
    rank_ref[...] = jnp.concatenate(
        [jnp.sum(jnp.where(hit, before, 0.0), axis=0, keepdims=True) for hit in hits], axis=0).astype(I32)
    carry_scr[...] = carry_scr[...] + jnp.sum(onehot, axis=1, keepdims=True)
    cnt_ref[...] = carry_scr[...]


def _router(geo, x, nw, mod, w_router_t, b_router, ks):
    t, d = x.shape
    n_e = w_router_t.shape[0]
    tm = _pick(512, geo.t_ctx, geo.seq_lat)
    row = geo.mod_row(tm)
    tri = (lax.broadcasted_iota(I32, (tm, tm), 0) < lax.broadcasted_iota(I32, (tm, tm), 1)).astype(BF16)
    return pl.pallas_call(
        functools.partial(_router_kernel, ks=ks),
        grid=(t // tm,),
        in_specs=[
            pl.BlockSpec((tm, d), lambda i: (i, 0)),
            pl.BlockSpec((1, d), lambda i: (0, 0)),
            pl.BlockSpec((1, 6, d), lambda i: (row(i), 0, 0)),
            pl.BlockSpec((n_e, d), lambda i: (0, 0)),
            pl.BlockSpec((n_e, 1), lambda i: (0, 0)),
            pl.BlockSpec((tm, tm), lambda i: (0, 0)),
        ],
        out_specs=[
            pl.BlockSpec((tm, d // 2), lambda i: (i, 0)),
            pl.BlockSpec((TOP_K, tm), lambda i: (0, i)),
            pl.BlockSpec((TOP_K, tm), lambda i: (0, i)),
            pl.BlockSpec((TOP_K, tm), lambda i: (0, i)),
            pl.BlockSpec((n_e, LANES), lambda i: (0, 0)),
        ],
        out_shape=[
            jax.ShapeDtypeStruct((t, d // 2), U32),
            jax.ShapeDtypeStruct((TOP_K, t), I32),
            jax.ShapeDtypeStruct((TOP_K, t), F32),
            jax.ShapeDtypeStruct((TOP_K, t), I32),
            jax.ShapeDtypeStruct((n_e, LANES), F32),
        ],
        scratch_shapes=[pltpu.VMEM((n_e, LANES), F32)],
        compiler_params=_cparams(("arbitrary",)),
        name="router",
    )(x, nw.reshape(1, d), mod, w_router_t, b_router.reshape(n_e, 1), tri)


def _dma_cparams(sem):
    return pltpu.CompilerParams(dimension_semantics=sem, vmem_limit_bytes=V7X_VMEM_LIMIT,
                                disable_bounds_checks=True)


def _dispatch_kernel(dest_ref, hp_ref, buf_ref, xb_ref, sem):
    del buf_ref
    tm = hp_ref.shape[0]

    def issue(t, carry):
        for k in range(TOP_K):
            pltpu.make_async_copy(hp_ref.at[pl.ds(t, 1)], xb_ref.at[pl.ds(dest_ref[k, t], 1)], sem).start()
        return carry

    lax.fori_loop(0, tm, issue, 0, unroll=8)
    for k in range(TOP_K):
        pltpu.make_async_copy(hp_ref, xb_ref.at[pl.ds(0, tm)], sem).wait()


def _dispatch(hp, dest, xb_buf):
    t, half = hp.shape
    tm = _pick(512, t)
    return pl.pallas_call(
        _dispatch_kernel,
        grid=(t // tm,),
        in_specs=[
            pl.BlockSpec((TOP_K, tm), lambda i: (0, i), memory_space=pltpu.SMEM),
            pl.BlockSpec((tm, half), lambda i: (i, 0)),
            pl.BlockSpec(memory_space=pl.ANY),
        ],
        out_specs=pl.BlockSpec(memory_space=pl.ANY),
        out_shape=jax.ShapeDtypeStruct(xb_buf.shape, xb_buf.dtype),
        input_output_aliases={2: 0},
        scratch_shapes=[pltpu.SemaphoreType.DMA],
        compiler_params=_dma_cparams(("arbitrary",)),
        name="dispatch",
    )(dest, hp, xb_buf)


def _expert_kernel(be_ref, nu_ref, nv_ref, xp_ref, wg_ref, bg_ref, wu_ref, bu_ref, wd_ref, bd_ref, o_ref, x_scr):
    i = pl.program_id(0)
    f = pl.program_id(1)
    half = xp_ref.shape[1]

    @pl.when(i >= nu_ref[0])
    def _():
        @pl.when(f == 0)
        def _():
            o_ref[...] = jnp.zeros(o_ref.shape, F32)

    @pl.when(i < nu_ref[0])
    def _():
        @pl.when(f == 0)
        def _():
            keep = lax.broadcasted_iota(I32, (xp_ref.shape[0], 1), 0) < nv_ref[i]
            lo, hi = _unpack_bf16_pairs(jnp.where(keep, xp_ref[...], jnp.uint32(0)))
            x_scr[:, :half] = lo
            x_scr[:, half:] = hi
            o_ref[...] = jnp.broadcast_to(bd_ref[0], o_ref.shape)

        x = x_scr[...]
        gate = jnp.minimum(_dot(x, wg_ref[...]) + bg_ref[0], SWIGLU_LIMIT)
        up = jnp.clip(_dot(x, wu_ref[...]) + bu_ref[0], -SWIGLU_LIMIT, SWIGLU_LIMIT)
        hdn = (up + 1.0) * (gate * jax.nn.sigmoid(SWIGLU_ALPHA * gate))
        o_ref[...] += _dot(hdn.astype(BF16), wd_ref[...])


def _experts(xb, block_e, n_used, n_valid, layer, w_gate, b_gate, w_up, b_up, w_down, b_down, tm):
    cap, half = xb.shape
    d = 2 * half
    n_e, _, ff = w_gate.shape[1:]
    tf = 512
    nb = cap // tm
    cl = lambda i, nu: jnp.minimum(i, nu[0] - 1)
    wspec_in = pl.BlockSpec((None, None, d, tf), lambda i, f, be, nu, nv: (layer, be[cl(i, nu)], 0, f))
    wspec_out = pl.BlockSpec((None, None, tf, d), lambda i, f, be, nu, nv: (layer, be[cl(i, nu)], f, 0))
    bspec_in = pl.BlockSpec((1, 1, tf), lambda i, f, be, nu, nv: (layer * n_e + be[cl(i, nu)], 0, f))
    bspec_out = pl.BlockSpec((1, 1, d), lambda i, f, be, nu, nv: (layer * n_e + be[cl(i, nu)], 0, 0))
    grid_spec = pltpu.PrefetchScalarGridSpec(
        num_scalar_prefetch=3,
        grid=(nb, ff // tf),
        in_specs=[
            pl.BlockSpec((tm, half), lambda i, f, be, nu, nv: (cl(i, nu), 0)),
            wspec_in, bspec_in, wspec_in, bspec_in, wspec_out, bspec_out,
        ],
        out_specs=pl.BlockSpec((tm, d), lambda i, f, be, nu, nv: (i, 0)),
        scratch_shapes=[pltpu.VMEM((tm, d), BF16)],
    )
    return pl.pallas_call(
        _expert_kernel,
        grid_spec=grid_spec,
        out_shape=jax.ShapeDtypeStruct((cap, d), F32),
        compiler_params=_cparams(("parallel", "arbitrary")),
        name="experts",
    )(block_e, n_used, n_valid, xb, w_gate, b_gate.reshape(-1, 1, ff), w_up, b_up.reshape(-1, 1, ff),
      w_down, b_down.reshape(-1, 1, d))


def _combine_kernel(dcur_ref, dnxt_ref, w_ref, x_ref, mod_ref, yb_ref, o_ref, ybuf, sems, *, kg):
    i = pl.program_id(0)
    n = pl.num_programs(0)
    tm = x_ref.shape[0]
    slot = i % 2

    def fetch(d_ref, s):
        def issue(t, carry):
            for k in range(TOP_K):
                pltpu.make_async_copy(yb_ref.at[pl.ds(d_ref[k, t], 1)], ybuf.at[s, k, pl.ds(t, 1)],
                                      sems.at[s]).start()
            return carry

        lax.fori_loop(0, tm, issue, 0, unroll=8)

    @pl.when(i == 0)
    def _():
        fetch(dcur_ref, 0)

    @pl.when(i + 1 < n)
    def _():
        fetch(dnxt_ref, 1 - slot)

    for k in range(TOP_K):
        pltpu.make_async_copy(yb_ref.at[pl.ds(0, tm)], ybuf.at[slot, k], sems.at[slot]).wait()
    w = w_ref[...]
    f = ybuf[slot, 0] * w[:, 0:1]
    for k in range(1, TOP_K):
        f = f + ybuf[slot, k] * w[:, k:k + 1]
    o_ref[...] = x_ref[...] + mod_ref[0][kg:kg + 1, :] * f


def _combine(geo, yb, dest, w_tk, x, mod, kg):
    t, d = x.shape
    tm = _pick(128, geo.t_ctx, geo.seq_lat)
    nsteps = t // tm
    row = geo.mod_row(tm)
    return pl.pallas_call(
        functools.partial(_combine_kernel, kg=kg),
        grid=(nsteps,),
        in_specs=[
            pl.BlockSpec((TOP_K, tm), lambda i: (0, i), memory_space=pltpu.SMEM),
            pl.BlockSpec((TOP_K, tm), lambda i: (0, jnp.minimum(i + 1, nsteps - 1)), memory_space=pltpu.SMEM),
            pl.BlockSpec((tm, TOP_K), lambda i: (i, 0)),
            pl.BlockSpec((tm, d), lambda i: (i, 0)),
            pl.BlockSpec((1, 6, d), lambda i: (row(i), 0, 0)),
            pl.BlockSpec(memory_space=pl.ANY),
        ],
        out_specs=pl.BlockSpec((tm, d), lambda i: (i, 0)),
        out_shape=jax.ShapeDtypeStruct((t, d), F32),
        scratch_shapes=[pltpu.VMEM((2, TOP_K, tm, d), F32), pltpu.SemaphoreType.DMA((2,))],
        compiler_params=_dma_cparams(("arbitrary",)),
        name="combine",
    )(dest, dest, w_tk, x, mod, yb)


MOE_TM = 1024


def _moe_blocks(t, n_e):
    return -(-(t * TOP_K) // MOE_TM) + n_e


def _moe(geo, x, nw, mod, layer, w_router, b_router, w_gate, b_gate, w_up, b_up, w_down, b_down, xb_buf):
    t, d = x.shape
    n_e = w_router.shape[1]
    hp, top_e, top_w, rank, cnt = _router(geo, x, nw, mod, w_router.T, b_router, 3)
    counts = cnt[:, 0].astype(I32)
    padded = (counts + MOE_TM - 1) // MOE_TM * MOE_TM
    pad_end = jnp.cumsum(padded)
    pad_start = pad_end - padded
    nb = _moe_blocks(t, n_e)
    experts = jnp.arange(n_e, dtype=I32)
    start_of = jnp.sum(jnp.where(top_e[None] == experts[:, None, None], pad_start[:, None, None], 0), axis=0)
    dest = start_of + rank
    blk0 = jnp.arange(nb, dtype=I32) * MOE_TM
    block_e = jnp.minimum(jnp.sum((pad_end[None, :] <= blk0[:, None]).astype(I32), axis=1), n_e - 1)
    own = block_e[:, None] == experts[None, :]
    group_end = jnp.sum(jnp.where(own, (pad_start + counts)[None, :], 0), axis=1)
    n_valid = jnp.clip(group_end - blk0, 0, MOE_TM).astype(I32)
    n_used = (pad_end[-1:] // MOE_TM).astype(I32)
    xb = _dispatch(hp, dest, xb_buf)
    yb = _experts(xb, block_e, n_used, n_valid, layer, w_gate, b_gate, w_up, b_up, w_down, b_down, MOE_TM)
    return _combine(geo, yb, dest, top_w.T, x, mod, 5), xb


_SWAP = np.array([(j + ROPE_FREQ) if (j // ROPE_FREQ) % 2 == 0 else (j - ROPE_FREQ) for j in range(QK_ROPE)])


def _rope_table(seq_lat, tq):
    pos = jnp.arange(seq_lat)
    row = (pos // GRID_W).astype(F32)
    colp = (pos % GRID_W).astype(F32)
    inv = jnp.power(ROPE_THETA, -jnp.arange(ROPE_FREQ, dtype=F32) / ROPE_FREQ)
    ar, ac = row[:, None] * inv, colp[:, None] * inv
    cos = jnp.concatenate([jnp.cos(ar), jnp.cos(ar), jnp.cos(ac), jnp.cos(ac)], axis=1)
    sin = jnp.concatenate([-jnp.sin(ar), jnp.sin(ar), -jnp.sin(ac), jnp.sin(ac)], axis=1)
    ident = jnp.concatenate([jnp.ones((tq, QK_ROPE), F32), jnp.zeros((tq, QK_ROPE), F32)], axis=1)
    return jnp.concatenate([ident, jnp.concatenate([cos, sin], axis=1)], axis=0)


def _even_params(w_in, w_qb, q_norm, q_rope_norm, k_rope_norm):
    d = w_in.shape[0]
    o_kv, o_kr, o_conv = Q_LORA, Q_LORA + KV_LORA, Q_LORA + KV_LORA + QK_ROPE
    kr = w_in[:, o_kr:o_conv]
    w_in_r = jnp.concatenate([w_in[:, o_conv:], w_in[:, :o_kr], kr, kr[:, _SWAP]], axis=1).astype(BF16)
    wq = w_qb.reshape(Q_LORA, MLA_HEADS, QK_NOPE + QK_ROPE)
    wq_r = jnp.concatenate([wq, wq[:, :, QK_NOPE:][:, :, _SWAP]], axis=2).reshape(Q_LORA, MLA_HEADS * 256)
    q_nw = jnp.concatenate([q_norm, q_rope_norm, q_rope_norm[_SWAP]]).reshape(1, 256)
    kr_nw = jnp.concatenate([k_rope_norm, k_rope_norm[_SWAP]]).reshape(1, 2 * QK_ROPE)
    del d
    return w_in_r, wq_r.astype(BF16), q_nw, kr_nw


def _odd_params(w_in, w_gk, b_gk):
    d = w_in.shape[0]
    pad = jnp.zeros((d, OD_N - w_in.shape[1]), w_in.dtype)
    w_in_r = jnp.concatenate([w_in, pad], axis=1).astype(BF16)
    wg = w_gk.reshape(2, GATE_RANK, GLA_HEADS, GLA_DKH).transpose(0, 2, 1, 3)
    wpad = jnp.zeros((2, GLA_HEADS, LANES, GLA_DKH), F32)
    wpad = wpad.at[0, :, :GATE_RANK].set(wg[0]).at[1, :, GATE_RANK:2 * GATE_RANK].set(wg[1])
    return w_in_r, wpad.astype(BF16), b_gk.reshape(2, GLA_HEADS, 1, GLA_DKH)


def kernel(x_prompt, x_sample, cache_mla_ckv, cache_mla_krope, state_gla, c, c_ctx, ada_w, ada_b, norm_mix_w, norm_ffn_w, ev_w_in, ev_q_a_norm, ev_w_qb, ev_kv_a_norm, ev_w_kvb, ev_q_norm, ev_k_norm, ev_q_rope_norm, ev_k_rope_norm, ev_conv_w, ev_w_out, od_w_in, od_w_gk, od_b_gk, od_o_norm, od_w_out, moe_w_router, moe_b_router, moe_w_gate, moe_b_gate, moe_w_up, moe_b_up, moe_w_down, moe_b_down):
    geo = _Geo(x_prompt, x_sample, cache_mla_ckv)
    d = geo.d
    depth = ada_w.shape[0]
    x = jnp.concatenate([x_prompt.reshape(geo.t_ctx, d), x_sample.reshape(geo.t_lat, d)], axis=0)
    cond = jnp.concatenate([c_ctx[None, :], c, jnp.zeros((16 - 1 - geo.n_lat, d), F32)], axis=0)
    mods = _adaln(cond, ada_w, ada_b).reshape(depth, 16, 6, d)
    tq = _pick(512, geo.t_ctx, geo.seq_lat)
    tab = _rope_table(geo.seq_lat, tq)
    w_gate, w_up, w_down = moe_w_gate.astype(BF16), moe_w_up.astype(BF16), moe_w_down.astype(BF16)
    xb_buf = jnp.zeros((_moe_blocks(geo.t, moe_w_router.shape[2]) * MOE_TM, d // 2), U32)
    attn_buf = jnp.zeros((geo.t, MLA_HEADS * V_HEAD), BF16)
    gla_bufs = (jnp.zeros((geo.t, GLA_HEADS * GLA_DVH), F32), jnp.zeros((geo.t, GLA_HEADS * GLA_DVH), F32))
    new_ckv, new_krope, new_gla = [], [], []
    for l in range(depth):
        i = l // 2
        mod = mods[l]
        if l % 2 == 0:
            w_in_r, wq_r, q_nw, kr_nw = _even_params(ev_w_in[i], ev_w_qb[i], ev_q_norm[i], ev_q_rope_norm[i],
                                                     ev_k_rope_norm[i])
            z = _lnmod_mm(geo, x, norm_mix_w[l], mod, w_in_r, 0)
            conv, qan, ckv, krn, krot = _even_mid(geo, z, ev_conv_w[i], ev_q_a_norm[i], ev_kv_a_norm[i], kr_nw,
                                                  tab, tq)
            q256 = _q_proj(geo, qan, wq_r, q_nw, tab, tq)
            kc = cache_mla_krope[:, i].reshape(geo.n_lat * geo.past, QK_ROPE)
            ckv_all = jnp.concatenate([ckv, cache_mla_ckv[:, i].reshape(geo.n_lat * geo.past, KV_LORA)], axis=0)
            krot_all = jnp.concatenate([krot, jnp.concatenate([kc, kc], axis=1).astype(BF16)], axis=0)
            k256, v256 = _kv_proj(ckv_all, ev_w_kvb[i].astype(BF16), ev_k_norm[i], krot_all)
            attn_buf = _attention(geo, q256, k256, v256, attn_buf)
            x = _mix_out(geo, attn_buf, conv, ev_w_out[i].astype(BF16), x, mod, 2)
            new_ckv.append(ckv[:geo.t_ctx].reshape(geo.n_ctx, geo.seq_ctx, KV_LORA))
            new_krope.append(krn[:geo.t_ctx].reshape(geo.n_ctx, geo.seq_ctx, QK_ROPE))
        else:
            w_in_r, wgk_pad, bgk = _odd_params(od_w_in[i], od_w_gk[i], od_b_gk[i])
            z = _lnmod_mm(geo, x, norm_mix_w[l], mod, w_in_r, 0)
            o_f, o_b, s_fin = _gla_scan(z, wgk_pad, bgk, geo.n_ctx, geo.seq_ctx, 0, None, True, gla_bufs)
            gla_bufs = _gla_scan(z, wgk_pad, bgk, geo.n_lat, geo.seq_lat, geo.t_ctx, state_gla[:, i], False,
                                 (o_f, o_b))
            x = _gla_out(geo, gla_bufs[0], gla_bufs[1], z, od_o_norm[i], od_w_out[i].astype(BF16), x, mod, 2)
            new_gla.append(s_fin)
        x, xb_buf = _moe(geo, x, norm_ffn_w[l], mod, l, moe_w_router[l], moe_b_router[l], w_gate, moe_b_gate,
                         w_up, moe_b_up, w_down, moe_b_down, xb_buf)
    xp = x[:geo.t_ctx].reshape(x_prompt.shape)
    xs = x[geo.t_ctx:].reshape(x_sample.shape)
    return (xp, xs, jnp.stack(new_ckv, axis=1), jnp.stack(new_krope, axis=1), jnp.stack(new_gla, axis=1))
```

```python
import functools

import jax
import jax.numpy as jnp
import numpy as np
from jax import lax
from jax.experimental import pallas as pl
from jax.experimental.pallas import tpu as pltpu

F32 = jnp.float32
BF16 = jnp.bfloat16
I32 = jnp.int32
U32 = jnp.uint32

EPS = 1e-6
GRID_W = 64
MLA_HEADS = 8
QK_NOPE = 128
QK_ROPE = 64
V_HEAD = 128
Q_LORA = 512
KV_LORA = 256
ROPE_THETA = 10000.0
ROPE_FREQ = QK_ROPE // 4
CONV_W = 1024
GLA_HEADS = 4
GLA_DKH = 256
GLA_DVH = 512
GATE_RANK = 16
GATE_NORM = 16.0
GLA_CHUNK = 64
TOP_K = 4
SWIGLU_ALPHA = 1.702
SWIGLU_LIMIT = 7.0

V7X_VMEM_LIMIT = 56 * 1024 * 1024
LANES = 128
HI = lax.Precision.HIGHEST


def _cparams(sem):
    return pltpu.CompilerParams(dimension_semantics=sem, vmem_limit_bytes=V7X_VMEM_LIMIT)


def _pick(want, *dims):
    t = want
    while any(d % t for d in dims):
        t //= 2
    return t


def _dot(a, b):
    return jnp.dot(a, b, preferred_element_type=F32)


def _rms(x, w):
    return x * lax.rsqrt(jnp.mean(x * x, axis=-1, keepdims=True) + EPS) * w


def _lnmod(x, nw, mod, ks):
    return _rms(x, nw) * (1.0 + mod[ks + 1:ks + 2, :]) + mod[ks:ks + 1, :]


class _Geo:
    def __init__(self, x_prompt, x_sample, cache_ckv):
        self.n_ctx, self.seq_ctx, self.d = x_prompt.shape
        self.n_lat, self.seq_lat, _ = x_sample.shape
        self.past = cache_ckv.shape[2]
        self.t_ctx = self.n_ctx * self.seq_ctx
        self.t_lat = self.n_lat * self.seq_lat
        self.t = self.t_ctx + self.t_lat

    def mod_row(self, tm):
        t_ctx, seq_lat = self.t_ctx, self.seq_lat
        return lambda i: jnp.where(i * tm < t_ctx, 0, 1 + (i * tm - t_ctx) // seq_lat)


def _adaln_kernel(c_ref, w_ref, b_ref, o_ref):
    c = c_ref[...]
    a = (c * jax.nn.sigmoid(c)).astype(BF16)
    o_ref[0] = _dot(a, w_ref[0].astype(BF16)) + b_ref[0]


def _adaln(cond16, ada_w, ada_b):
    depth, d, n = ada_w.shape
    tn = 1024
    return pl.pallas_call(
        _adaln_kernel,
        grid=(depth, n // tn),
        in_specs=[
            pl.BlockSpec((16, d), lambda l, j: (0, 0)),
            pl.BlockSpec((1, d, tn), lambda l, j: (l, 0, j)),
            pl.BlockSpec((1, 1, tn), lambda l, j: (l, 0, j)),
        ],
        out_specs=pl.BlockSpec((1, 16, tn), lambda l, j: (l, 0, j)),
        out_shape=jax.ShapeDtypeStruct((depth, 16, n), F32),
        compiler_params=_cparams(("parallel", "parallel")),
        name="adaln",
    )(cond16, ada_w, ada_b.reshape(depth, 1, n))


def _lnmod_mm_kernel(x_ref, nw_ref, mod_ref, w_ref, o_ref, h_scr, *, ks):
    @pl.when(pl.program_id(1) == 0)
    def _():
        h_scr[...] = _lnmod(x_ref[...], nw_ref[...], mod_ref[0], ks).astype(BF16)

    o_ref[...] = _dot(h_scr[...], w_ref[...])


def _lnmod_mm(geo, x, nw, mod, w, ks):
    t, d = x.shape
    n = w.shape[1]
    tm = _pick(1024, geo.t_ctx, geo.seq_lat)
    tn = max(c for c in range(LANES, 1024 + 1, LANES) if n % c == 0)
    row = geo.mod_row(tm)
    return pl.pallas_call(
        functools.partial(_lnmod_mm_kernel, ks=ks),
        grid=(t // tm, n // tn),
        in_specs=[
            pl.BlockSpec((tm, d), lambda i, j: (i, 0)),
            pl.BlockSpec((1, d), lambda i, j: (0, 0)),
            pl.BlockSpec((1, 6, d), lambda i, j: (row(i), 0, 0)),
            pl.BlockSpec((d, tn), lambda i, j: (0, j)),
        ],
        out_specs=pl.BlockSpec((tm, tn), lambda i, j: (i, j)),
        out_shape=jax.ShapeDtypeStruct((t, n), F32),
        scratch_shapes=[pltpu.VMEM((tm, d), BF16)],
        compiler_params=_cparams(("parallel", "arbitrary")),
        name="lnmod_mm",
    )(x, nw.reshape(1, d), mod, w)


EV_QA = 3 * CONV_W
EV_KVA = EV_QA + Q_LORA
EV_KR = EV_KVA + KV_LORA
EV_N = 4096


def _even_mid_kernel(bg_ref, cg_ref, xv_ref, cgp_ref, xvp_ref, cgn_ref, xvn_ref, qa_ref, kva_ref, kr_ref,
                     cw_ref, qaw_ref, kvw_ref, krw_ref, tab_ref,
                     conv_ref, qan_ref, ckv_ref, krn_ref, krot_ref, *, t_ctx, seq_ctx, seq_lat):
    i = pl.program_id(0)
    tm = bg_ref.shape[0]
    u = cg_ref[...] * xv_ref[...]
    rows = lax.broadcasted_iota(I32, (tm, 1), 0)
    tok = i * tm + rows
    pos = jnp.where(tok < t_ctx, tok & (seq_ctx - 1), (tok - t_ctx) & (seq_lat - 1))
    last = jnp.where(tok < t_ctx, seq_ctx - 1, seq_lat - 1)
    u_m1 = jnp.where(rows == 0, cgp_ref[7:8, :] * xvp_ref[7:8, :], pltpu.roll(u, 1, axis=0))
    u_p1 = jnp.where(rows == tm - 1, cgn_ref[0:1, :] * xvn_ref[0:1, :], pltpu.roll(u, tm - 1, axis=0))
    u_m1 = jnp.where(pos == 0, 0.0, u_m1)
    u_p1 = jnp.where(pos == last, 0.0, u_p1)
    cw = cw_ref[...]
    conv = bg_ref[...] * (u_m1 * cw[0:1, :] + u * cw[1:2, :] + u_p1 * cw[2:3, :])
    conv_ref[...] = conv.astype(BF16)
    qan_ref[...] = _rms(qa_ref[...], qaw_ref[...]).astype(BF16)
    ckv_ref[...] = _rms(kva_ref[...], kvw_ref[...])
    kr = kr_ref[...]
    krn = _rms(kr, krw_ref[...])
    krn_ref[...] = krn[:, :QK_ROPE]
    y = krn * tab_ref[...]
    krot_ref[...] = (y + pltpu.roll(y, QK_ROPE, axis=1)).astype(BF16)


def _even_mid(geo, z, conv_w, qa_w, kv_w, kr_w128, tab, tq):
    t = z.shape[0]
    tm = tq
    ncb = geo.t_ctx // tm
    bps = geo.seq_lat // tm
    nb8 = t // 8
    r8 = tm // 8
    assert geo.seq_ctx & (geo.seq_ctx - 1) == 0 and geo.seq_lat & (geo.seq_lat - 1) == 0
    tab_idx = lambda i: jnp.where(i < ncb, 0, 1 + (i - ncb) % bps)
    col = lambda c: (lambda i: (i, c))
    in_specs = [
        pl.BlockSpec((tm, CONV_W), col(0)),
        pl.BlockSpec((tm, CONV_W), col(1)),
        pl.BlockSpec((tm, CONV_W), col(2)),
        pl.BlockSpec((8, CONV_W), lambda i: (jnp.maximum(i * r8 - 1, 0), 1)),
        pl.BlockSpec((8, CONV_W), lambda i: (jnp.maximum(i * r8 - 1, 0), 2)),
        pl.BlockSpec((8, CONV_W), lambda i: (jnp.minimum((i + 1) * r8, nb8 - 1), 1)),
        pl.BlockSpec((8, CONV_W), lambda i: (jnp.minimum((i + 1) * r8, nb8 - 1), 2)),
        pl.BlockSpec((tm, Q_LORA), col(EV_QA // Q_LORA)),
        pl.BlockSpec((tm, KV_LORA), col(EV_KVA // KV_LORA)),
        pl.BlockSpec((tm, 2 * QK_ROPE), col(EV_KR // (2 * QK_ROPE))),
        pl.BlockSpec((3, CONV_W), lambda i: (0, 0)),
        pl.BlockSpec((1, Q_LORA), lambda i: (0, 0)),
        pl.BlockSpec((1, KV_LORA), lambda i: (0, 0)),
        pl.BlockSpec((1, 2 * QK_ROPE), lambda i: (0, 0)),
        pl.BlockSpec((tm, 2 * QK_ROPE), lambda i: (tab_idx(i), 0)),
    ]
    out_specs = [
        pl.BlockSpec((tm, CONV_W), lambda i: (i, 0)),
        pl.BlockSpec((tm, Q_LORA), lambda i: (i, 0)),
        pl.BlockSpec((tm, KV_LORA), lambda i: (i, 0)),
        pl.BlockSpec((tm, QK_ROPE), lambda i: (i, 0)),
        pl.BlockSpec((tm, 2 * QK_ROPE), lambda i: (i, 0)),
    ]
    out_shape = [
        jax.ShapeDtypeStruct((t, CONV_W), BF16),
        jax.ShapeDtypeStruct((t, Q_LORA), BF16),
        jax.ShapeDtypeStruct((t, KV_LORA), F32),
        jax.ShapeDtypeStruct((t, QK_ROPE), F32),
        jax.ShapeDtypeStruct((t, 2 * QK_ROPE), BF16),
    ]
    return pl.pallas_call(
        functools.partial(_even_mid_kernel, t_ctx=geo.t_ctx, seq_ctx=geo.seq_ctx, seq_lat=geo.seq_lat),
        grid=(t // tm,),
        in_specs=in_specs,
        out_specs=out_specs,
        out_shape=out_shape,
        compiler_params=_cparams(("parallel",)),
        name="even_mid",
    )(z, z, z, z, z, z, z, z, z, z, conv_w, qa_w.reshape(1, -1), kv_w.reshape(1, -1), kr_w128, tab)


def _q_proj_kernel(a_ref, w_ref, nw_ref, tab_ref, o_ref):
    acc = _dot(a_ref[...], w_ref[...])
    nw = nw_ref[...]
    tab = tab_ref[...]
    for h in range(MLA_HEADS):
        n = acc[:, h * 256:h * 256 + QK_NOPE]
        r = acc[:, h * 256 + QK_NOPE:(h + 1) * 256]
        qn = _rms(n, nw[:, :QK_NOPE])
        y = _rms(r, nw[:, QK_NOPE:]) * tab
        o_ref[h] = (jnp.concatenate([qn, y], axis=-1) * Q_PRESCALE).astype(BF16)


def _q_proj(geo, qan, w_qb, nw256, tab, tq):
    t = qan.shape[0]
    tm = tq
    ncb = geo.t_ctx // tm
    bps = geo.seq_lat // tm
    n = w_qb.shape[1]
    return pl.pallas_call(
        _q_proj_kernel,
        grid=(t // tm,),
        in_specs=[
            pl.BlockSpec((tm, Q_LORA), lambda i: (i, 0)),
            pl.BlockSpec((Q_LORA, n), lambda i: (0, 0)),
            pl.BlockSpec((1, 256), lambda i: (0, 0)),
            pl.BlockSpec((tm, 2 * QK_ROPE), lambda i: (jnp.where(i < ncb, 0, 1 + (i - ncb) % bps), 0)),
        ],
        out_specs=pl.BlockSpec((MLA_HEADS, tm, 256), lambda i: (0, i, 0)),
        out_shape=jax.ShapeDtypeStruct((MLA_HEADS, t, 256), BF16),
        compiler_params=_cparams(("parallel",)),
        name="q_proj",
    )(qan, w_qb, nw256, tab)


def _kv_proj_kernel(a_ref, w_ref, nw_ref, kr_ref, k_ref, v_ref):
    acc = _dot(a_ref[...].astype(BF16), w_ref[...])
    kr = kr_ref[...]
    ones = jnp.ones((acc.shape[0], V_HEAD), BF16)
    for h in range(MLA_HEADS):
        kn = _rms(acc[:, h * 256:h * 256 + QK_NOPE], nw_ref[...]).astype(BF16)
        k_ref[h] = jnp.concatenate([kn, kr], axis=-1)
        v_ref[h] = jnp.concatenate([acc[:, h * 256 + QK_NOPE:(h + 1) * 256].astype(BF16), ones], axis=-1)


def _kv_proj(ckv_all, w_kvb, k_norm, krot_all):
    t = ckv_all.shape[0]
    tm = _pick(512, t)
    n = w_kvb.shape[1]
    return pl.pallas_call(
        _kv_proj_kernel,
        grid=(t // tm,),
        in_specs=[
            pl.BlockSpec((tm, KV_LORA), lambda i: (i, 0)),
            pl.BlockSpec((KV_LORA, n), lambda i: (0, 0)),
            pl.BlockSpec((1, QK_NOPE), lambda i: (0, 0)),
            pl.BlockSpec((tm, 2 * QK_ROPE), lambda i: (i, 0)),
        ],
        out_specs=[
            pl.BlockSpec((MLA_HEADS, tm, 256), lambda i: (0, i, 0)),
            pl.BlockSpec((MLA_HEADS, tm, 2 * V_HEAD), lambda i: (0, i, 0)),
        ],
        out_shape=[
            jax.ShapeDtypeStruct((MLA_HEADS, t, 256), BF16),
            jax.ShapeDtypeStruct((MLA_HEADS, t, 2 * V_HEAD), BF16),
        ],
        compiler_params=_cparams(("parallel",)),
        name="kv_proj",
    )(ckv_all, w_kvb, k_norm.reshape(1, -1), krot_all)


_NT = (((1,), (1,)), ((), ()))
_TN = (((0,), (0,)), ((), ()))
Q_PRESCALE = (QK_NOPE + QK_ROPE) ** -0.5 * 1.4426950408889634
ATTN_KEY_CHUNK = 512


def _softmax_step(q, k, v, m, acc):
    s = lax.dot_general(q, k, _NT, preferred_element_type=F32)
    m_new = jnp.maximum(m, jnp.max(s, axis=-1, keepdims=True))
    p = jnp.exp2(s - m_new)
    acc = acc * jnp.exp2(m - m_new) + _dot(p.astype(BF16), v)
    return m_new, acc


def _attn_ctx_kernel(q_ref, k_ref, v_ref, prev_ref, o_ref):
    del prev_ref
    outs = []
    for h in range(MLA_HEADS):
        s = lax.dot_general(q_ref[h], k_ref[h], _NT, preferred_element_type=F32)
        p = jnp.exp2(s - jnp.max(s, axis=-1, keepdims=True))
        o = _dot(p.astype(BF16), v_ref[h])
        outs.append(o[:, :V_HEAD] / o[:, V_HEAD:V_HEAD + 1])
    o_ref[...] = jnp.concatenate(outs, axis=-1).astype(BF16)


def _attn_lat_kernel(q_ref, kc_ref, vc_ref, kl_ref, vl_ref, prev_ref, o_ref):
    del prev_ref
    q = q_ref[0]
    tq = q.shape[0]
    m = jnp.full((tq, 1), -jnp.inf, F32)
    acc = jnp.zeros((tq, 2 * V_HEAD), F32)
    for k_ref, v_ref in ((kc_ref, vc_ref), (kl_ref, vl_ref)):
        n = k_ref.shape[1]
        ck = min(ATTN_KEY_CHUNK, n)
        for j in range(n // ck):
            m, acc = _softmax_step(q, k_ref[0, j * ck:(j + 1) * ck, :], v_ref[0, j * ck:(j + 1) * ck, :], m, acc)
    o_ref[...] = (acc[:, :V_HEAD] / acc[:, V_HEAD:V_HEAD + 1]).astype(BF16)


def _attention(geo, q256, k256, v256, o_buf):
    t = geo.t
    sc, sl, past = geo.seq_ctx, geo.seq_lat, geo.past
    hv = MLA_HEADS * V_HEAD
    vw = 2 * V_HEAD
    o_ctx = pl.pallas_call(
        _attn_ctx_kernel,
        grid=(geo.n_ctx,),
        in_specs=[
            pl.BlockSpec((MLA_HEADS, sc, 256), lambda b: (0, b, 0)),
            pl.BlockSpec((MLA_HEADS, sc, 256), lambda b: (0, b, 0)),
            pl.BlockSpec((MLA_HEADS, sc, vw), lambda b: (0, b, 0)),
            pl.BlockSpec(memory_space=pl.ANY),
        ],
        out_specs=pl.BlockSpec((sc, hv), lambda b: (b, 0)),
        out_shape=jax.ShapeDtypeStruct((t, hv), BF16),
        input_output_aliases={3: 0},
        compiler_params=_cparams(("parallel",)),
        name="attn_ctx",
    )(q256, k256, v256, o_buf)
    tq = _pick(512, sl)
    nq = sl // tq
    q_off = geo.t_ctx // tq
    kl_off = geo.t_ctx // sl
    kc_off = t // past
    return pl.pallas_call(
        _attn_lat_kernel,
        grid=(geo.n_lat, MLA_HEADS, nq),
        in_specs=[
            pl.BlockSpec((1, tq, 256), lambda b, h, i: (h, q_off + b * nq + i, 0)),
            pl.BlockSpec((1, past, 256), lambda b, h, i: (h, kc_off + b, 0)),
            pl.BlockSpec((1, past, vw), lambda b, h, i: (h, kc_off + b, 0)),
            pl.BlockSpec((1, sl, 256), lambda b, h, i: (h, kl_off + b, 0)),
            pl.BlockSpec((1, sl, vw), lambda b, h, i: (h, kl_off + b, 0)),
            pl.BlockSpec(memory_space=pl.ANY),
        ],
        out_specs=pl.BlockSpec((tq, V_HEAD), lambda b, h, i: (q_off + b * nq + i, h)),
        out_shape=jax.ShapeDtypeStruct((t, hv), BF16),
        input_output_aliases={5: 0},
        compiler_params=_cparams(("parallel", "parallel", "arbitrary")),
        name="attn_lat",
    )(q256, k256, v256, k256, v256, o_ctx)


def _mix_out_kernel(a1_ref, a2_ref, w_ref, x_ref, mod_ref, o_ref, *, kg):
    k1 = a1_ref.shape[1]
    acc = _dot(a1_ref[...], w_ref[:k1, :]) + _dot(a2_ref[...], w_ref[k1:, :])
    o_ref[...] = x_ref[...] + mod_ref[0][kg:kg + 1, :] * acc


def _mix_out(geo, a1, a2, w, x, mod, kg):
    t, d = x.shape
    tm = _pick(512, geo.t_ctx, geo.seq_lat)
    row = geo.mod_row(tm)
    k1, k2 = a1.shape[1], a2.shape[1]
    return pl.pallas_call(
        functools.partial(_mix_out_kernel, kg=kg),
        grid=(t // tm,),
        in_specs=[
            pl.BlockSpec((tm, k1), lambda i: (i, 0)),
            pl.BlockSpec((tm, k2), lambda i: (i, 0)),
            pl.BlockSpec((k1 + k2, d), lambda i: (0, 0)),
            pl.BlockSpec((tm, d), lambda i: (i, 0)),
            pl.BlockSpec((1, 6, d), lambda i: (row(i), 0, 0)),
        ],
        out_specs=pl.BlockSpec((tm, d), lambda i: (i, 0)),
        out_shape=jax.ShapeDtypeStruct((t, d), F32),
        compiler_params=_cparams(("parallel",)),
        name="mix_out",
    )(a1, a2, w, x, mod)


OD_K = GLA_HEADS * GLA_DKH
OD_V = 2 * OD_K
OD_G = OD_V + GLA_HEADS * GLA_DVH
OD_R = OD_G + GLA_HEADS * GLA_DVH
OD_N = OD_R + LANES


def _log_sigmoid(x):
    return jnp.minimum(x, 0.0) - jnp.log1p(jnp.exp(-jnp.abs(x)))


def _split3(x):
    hi = x.astype(BF16)
    r1 = x - hi.astype(F32)
    mid = r1.astype(BF16)
    lo = (r1 - mid.astype(F32)).astype(BF16)
    return hi, mid, lo


def _gla_chunk(q, k, v, r, wgk, bgk, st_ref, o_ref, backward):
    c = q.shape[0]
    la = _log_sigmoid(_dot(r.astype(BF16), wgk) + bgk) * (1.0 / GATE_NORM)
    ii = lax.broadcasted_iota(I32, (c, c), 0)
    jj = lax.broadcasted_iota(I32, (c, c), 1)
    causal = (ii <= jj) if backward else (ii >= jj)
    tri = causal.astype(BF16)
    hi, mid, lo = _split3(la)
    b = _dot(tri, hi) + _dot(tri, mid) + _dot(tri, lo)
    tot = jnp.sum(la, axis=0, keepdims=True)
    q_dec = (q * (GLA_DKH ** -0.5) * jnp.exp(b)).astype(BF16)
    k_dec = (k * jnp.exp(-b)).astype(BF16)
    k_end = (k * jnp.exp(tot - b)).astype(BF16)
    e_tot = jnp.exp(tot)
    vb = v.astype(BF16)
    heads = range(GLA_HEADS)
    ks = [slice(h * GLA_DKH, (h + 1) * GLA_DKH) for h in heads]
    vs = [slice(h * GLA_DVH, (h + 1) * GLA_DVH) for h in heads]
    scores = [jnp.where(causal, lax.dot_general(q_dec[:, ks[h]], k_dec[:, ks[h]], _NT, preferred_element_type=F32),
                        0.0).astype(BF16) for h in heads]
    st = [st_ref[h] for h in heads]
    for h in heads:
        o_ref[:, vs[h]] = (_dot(scores[h], vb[:, vs[h]]) + lax.dot_general(
            q_dec[:, ks[h]], st[h].astype(BF16), _NT, preferred_element_type=F32)).astype(o_ref.dtype)
    for h in heads:
        st_ref[h] = e_tot[:, ks[h]] * st[h] + lax.dot_general(vb[:, vs[h]], k_end[:, ks[h]], _TN,
                                                             preferred_element_type=F32)


def _gla_kernel(*refs, has_init, want_final):
    qf, kf, vf, rf, qb, kb, vb, rb, wgk_ref, bgk_ref = refs[:10]
    pos = 10
    s0_ref = None
    if has_init:
        s0_ref = refs[pos]
        pos += 1
    pos += 2
    of_ref, ob_ref = refs[pos], refs[pos + 1]
    pos += 2
    sfin_ref = None
    if want_final:
        sfin_ref = refs[pos]
        pos += 1
    st_scr = refs[pos]
    c = pl.program_id(1)
    nc = pl.num_programs(1)

    @pl.when(c == 0)
    def _():
        for d in range(2):
            for h in range(GLA_HEADS):
                if has_init:
                    st_scr[d, h] = s0_ref[0, d, h].T
                else:
                    st_scr[d, h] = jnp.zeros(st_scr.shape[2:], F32)

    _gla_chunk(qf[...], kf[...], vf[...], rf[...], wgk_ref[0], bgk_ref[0], st_scr.at[0], of_ref, False)
    _gla_chunk(qb[...], kb[...], vb[...], rb[...], wgk_ref[1], bgk_ref[1], st_scr.at[1], ob_ref, True)

    if want_final:
        @pl.when(c == nc - 1)
        def _():
            for d in range(2):
                for h in range(GLA_HEADS):
                    sfin_ref[0, d, h] = st_scr[d, h].T


def _gla_scan(z, wgk_pad, bgk, n_seq, seq, row0, s0, want_final, bufs):
    t = z.shape[0]
    c = GLA_CHUNK
    nc = seq // c
    base = row0 // c
    fwd = lambda b, i: base + b * nc + i
    bwd = lambda b, i: base + b * nc + (nc - 1 - i)
    dk, dv = GLA_HEADS * GLA_DKH, GLA_HEADS * GLA_DVH

    def zspecs(rowf):
        return [
            pl.BlockSpec((c, dk), lambda b, i: (rowf(b, i), 0)),
            pl.BlockSpec((c, dk), lambda b, i: (rowf(b, i), OD_K // dk)),
            pl.BlockSpec((c, dv), lambda b, i: (rowf(b, i), OD_V // dv)),
            pl.BlockSpec((c, LANES), lambda b, i: (rowf(b, i), OD_R // LANES)),
        ]

    in_specs = zspecs(fwd) + zspecs(bwd) + [
        pl.BlockSpec((2, LANES, dk), lambda b, i: (0, 0, 0)),
        pl.BlockSpec((2, 1, dk), lambda b, i: (0, 0, 0)),
    ]
    args = [z] * 8 + [wgk_pad, bgk]
    if s0 is not None:
        in_specs.append(pl.BlockSpec((1, 2, GLA_HEADS, GLA_DKH, GLA_DVH), lambda b, i: (b, 0, 0, 0, 0)))
        args.append(s0)
    aliases = {len(args): 0, len(args) + 1: 1}
    in_specs += [pl.BlockSpec(memory_space=pl.ANY)] * 2
    args += list(bufs)
    out_specs = [pl.BlockSpec((c, dv), lambda b, i: (fwd(b, i), 0)), pl.BlockSpec((c, dv), lambda b, i: (bwd(b, i), 0))]
    out_shape = [jax.ShapeDtypeStruct((t, dv), bufs[0].dtype)] * 2
    if want_final:
        out_specs.append(pl.BlockSpec((1, 2, GLA_HEADS, GLA_DKH, GLA_DVH), lambda b, i: (b, 0, 0, 0, 0)))
        out_shape.append(jax.ShapeDtypeStruct((n_seq, 2, GLA_HEADS, GLA_DKH, GLA_DVH), F32))
    return pl.pallas_call(
        functools.partial(_gla_kernel, has_init=s0 is not None, want_final=want_final),
        grid=(n_seq, nc),
        in_specs=in_specs,
        out_specs=out_specs,
        out_shape=out_shape,
        input_output_aliases=aliases,
        scratch_shapes=[pltpu.VMEM((2, GLA_HEADS, GLA_DVH, GLA_DKH), F32)],
        compiler_params=_cparams(("parallel", "arbitrary")),
        name="gla_scan",
    )(*args)


def _gla_out_kernel(of_ref, ob_ref, g_ref, nw_ref, w_ref, x_ref, mod_ref, o_ref, a_scr, *, kg):
    @pl.when(pl.program_id(1) == 0)
    def _():
        nw = nw_ref[...]
        for h in range(GLA_HEADS):
            sl = slice(h * GLA_DVH, (h + 1) * GLA_DVH)
            o = _rms(of_ref[:, sl].astype(F32) + ob_ref[:, sl].astype(F32), nw)
            g = g_ref[:, sl]
            a_scr[:, sl] = (o * (g * jax.nn.sigmoid(g))).astype(BF16)

    o_ref[...] = x_ref[...] + mod_ref[0][kg:kg + 1, :] * _dot(a_scr[...], w_ref[...])


def _gla_out(geo, o_f, o_b, z, o_norm, w_out, x, mod, kg):
    t, d = x.shape
    dv = GLA_HEADS * GLA_DVH
    tm = _pick(512, geo.t_ctx, geo.seq_lat)
    tn = 512
    row = geo.mod_row(tm)
    return pl.pallas_call(
        functools.partial(_gla_out_kernel, kg=kg),
        grid=(t // tm, d // tn),
        in_specs=[
            pl.BlockSpec((tm, dv), lambda i, j: (i, 0)),
            pl.BlockSpec((tm, dv), lambda i, j: (i, 0)),
            pl.BlockSpec((tm, dv), lambda i, j: (i, OD_G // dv)),
            pl.BlockSpec((1, GLA_DVH), lambda i, j: (0, 0)),
            pl.BlockSpec((dv, tn), lambda i, j: (0, j)),
            pl.BlockSpec((tm, tn), lambda i, j: (i, j)),
            pl.BlockSpec((1, 6, tn), lambda i, j: (row(i), 0, j)),
        ],
        out_specs=pl.BlockSpec((tm, tn), lambda i, j: (i, j)),
        out_shape=jax.ShapeDtypeStruct((t, d), F32),
        scratch_shapes=[pltpu.VMEM((tm, dv), BF16)],
        compiler_params=_cparams(("parallel", "arbitrary")),
        name="gla_out",
    )(o_f, o_b, z, o_norm.reshape(1, -1), w_out, x, mod)


def _pack_bf16_pairs(h):
    n = h.shape[1] // 2
    lo = pltpu.bitcast(h[:, :n].astype(BF16).astype(F32), U32)
    hi = pltpu.bitcast(h[:, n:].astype(BF16).astype(F32), U32)
    return (lo >> 16) | (hi & jnp.uint32(0xFFFF0000))


def _unpack_bf16_pairs(w):
    lo = pltpu.bitcast(w << 16, F32).astype(BF16)
    hi = pltpu.bitcast(w & jnp.uint32(0xFFFF0000), F32).astype(BF16)
    return lo, hi


def _router_kernel(x_ref, nw_ref, mod_ref, wr_ref, br_ref, tri_ref,
                   hp_ref, e_ref, w_ref, rank_ref, cnt_ref, carry_scr, *, ks):
    @pl.when(pl.program_id(0) == 0)
    def _():
        carry_scr[...] = jnp.zeros(carry_scr.shape, F32)

    h = _lnmod(x_ref[...], nw_ref[...], mod_ref[0], ks)
    hp_ref[...] = _pack_bf16_pairs(h)
    lg = lax.dot_general(wr_ref[...], h, _NT, precision=HI, preferred_element_type=F32) + br_ref[...]
    n_e, tm = lg.shape
    rows = lax.broadcasted_iota(I32, (n_e, tm), 0).astype(F32)
    vals, sels, hits = [], [], []
    for _ in range(TOP_K):
        m = jnp.max(lg, axis=0, keepdims=True)
        idx = jnp.min(jnp.where(lg == m, rows, float(n_e)), axis=0, keepdims=True)
        hit = rows == idx
        vals.append(m)
        sels.append(idx)
        hits.append(hit)
        lg = jnp.where(hit, -jnp.inf, lg)
    ex = [jnp.exp(v - vals[0]) for v in vals]
    den = ex[0] + ex[1] + ex[2] + ex[3]
    w_ref[...] = jnp.concatenate([e / den for e in ex], axis=0)
    e_ref[...] = jnp.concatenate(sels, axis=0).astype(I32)
    onehot = jnp.zeros((n_e, tm), F32)
    for hit in hits:
        onehot = jnp.where(hit, 1.0, onehot)
    before = carry_scr[:, 0:1] + _dot(onehot.astype(BF16), tri_ref[...])
    rank_ref[...] = jnp.concatenate(
        [jnp.sum(jnp.where(hit, before, 0.0), axis=0, keepdims=True) for hit in hits], axis=0).astype(I32)
    carry_scr[...] = carry_scr[...] + jnp.sum(onehot, axis=1, keepdims=True)
    cnt_ref[...] = carry_scr[...]


def _router(geo, x, nw, mod, w_router_t, b_router, ks):
    t, d = x.shape
    n_e = w_router_t.shape[0]
    tm = _pick(512, geo.t_ctx, geo.seq_lat)
    row = geo.mod_row(tm)
    tri = (lax.broadcasted_iota(I32, (tm, tm), 0) < lax.broadcasted_iota(I32, (tm, tm), 1)).astype(BF16)
    return pl.pallas_call(
        functools.partial(_router_kernel, ks=ks),
        grid=(t // tm,),
        in_specs=[
            pl.BlockSpec((tm, d), lambda i: (i, 0)),
            pl.BlockSpec((1, d), lambda i: (0, 0)),
            pl.BlockSpec((1, 6, d), lambda i: (row(i), 0, 0)),
            pl.BlockSpec((n_e, d), lambda i: (0, 0)),
            pl.BlockSpec((n_e, 1), lambda i: (0, 0)),
            pl.BlockSpec((tm, tm), lambda i: (0, 0)),
        ],
        out_specs=[
            pl.BlockSpec((tm, d // 2), lambda i: (i, 0)),
            pl.BlockSpec((TOP_K, tm), lambda i: (0, i)),
            pl.BlockSpec((TOP_K, tm), lambda i: (0, i)),
            pl.BlockSpec((TOP_K, tm), lambda i: (0, i)),
            pl.BlockSpec((n_e, LANES), lambda i: (0, 0)),
        ],
        out_shape=[
            jax.ShapeDtypeStruct((t, d // 2), U32),
            jax.ShapeDtypeStruct((TOP_K, t), I32),
            jax.ShapeDtypeStruct((TOP_K, t), F32),
            jax.ShapeDtypeStruct((TOP_K, t), I32),
            jax.ShapeDtypeStruct((n_e, LANES), F32),
        ],
        scratch_shapes=[pltpu.VMEM((n_e, LANES), F32)],
        compiler_params=_cparams(("arbitrary",)),
        name="router",
    )(x, nw.reshape(1, d), mod, w_router_t, b_router.reshape(n_e, 1), tri)


def _dma_cparams(sem):
    return pltpu.CompilerParams(dimension_semantics=sem, vmem_limit_bytes=V7X_VMEM_LIMIT,
                                disable_bounds_checks=True)


def _dispatch_kernel(dest_ref, hp_ref, buf_ref, xb_ref, sem):
    del buf_ref
    tm = hp_ref.shape[0]

    def issue(t, carry):
        for k in range(TOP_K):
            pltpu.make_async_copy(hp_ref.at[pl.ds(t, 1)], xb_ref.at[pl.ds(dest_ref[k, t], 1)], sem).start()
        return carry

    lax.fori_loop(0, tm, issue, 0, unroll=8)
    for k in range(TOP_K):
        pltpu.make_async_copy(hp_ref, xb_ref.at[pl.ds(0, tm)], sem).wait()


def _dispatch(hp, dest, xb_buf):
    t, half = hp.shape
    tm = _pick(512, t)
    return pl.pallas_call(
        _dispatch_kernel,
        grid=(t // tm,),
        in_specs=[
            pl.BlockSpec((TOP_K, tm), lambda i: (0, i), memory_space=pltpu.SMEM),
            pl.BlockSpec((tm, half), lambda i: (i, 0)),
            pl.BlockSpec(memory_space=pl.ANY),
        ],
        out_specs=pl.BlockSpec(memory_space=pl.ANY),
        out_shape=jax.ShapeDtypeStruct(xb_buf.shape, xb_buf.dtype),
        input_output_aliases={2: 0},
        scratch_shapes=[pltpu.SemaphoreType.DMA],
        compiler_params=_dma_cparams(("arbitrary",)),
        name="dispatch",
    )(dest, hp, xb_buf)


def _expert_kernel(be_ref, nu_ref, nv_ref, xp_ref, wg_ref, bg_ref, wu_ref, bu_ref, wd_ref, bd_ref, o_ref, x_scr):
    i = pl.program_id(0)
    f = pl.program_id(1)
    half = xp_ref.shape[1]

    @pl.when(i >= nu_ref[0])
    def _():
        @pl.when(f == 0)
        def _():
            o_ref[...] = jnp.zeros(o_ref.shape, F32)

    @pl.when(i < nu_ref[0])
    def _():
        @pl.when(f == 0)
        def _():
            keep = lax.broadcasted_iota(I32, (xp_ref.shape[0], 1), 0) < nv_ref[i]
            lo, hi = _unpack_bf16_pairs(jnp.where(keep, xp_ref[...], jnp.uint32(0)))
            x_scr[:, :half] = lo
            x_scr[:, half:] = hi
            o_ref[...] = jnp.broadcast_to(bd_ref[0], o_ref.shape)

        x = x_scr[...]
        gate = jnp.minimum(_dot(x, wg_ref[...]) + bg_ref[0], SWIGLU_LIMIT)
        up = jnp.clip(_dot(x, wu_ref[...]) + bu_ref[0], -SWIGLU_LIMIT, SWIGLU_LIMIT)
        hdn = (up + 1.0) * (gate * jax.nn.sigmoid(SWIGLU_ALPHA * gate))
        o_ref[...] += _dot(hdn.astype(BF16), wd_ref[...])


def _experts(xb, block_e, n_used, n_valid, layer, w_gate, b_gate, w_up, b_up, w_down, b_down, tm):
    cap, half = xb.shape
    d = 2 * half
    n_e, _, ff = w_gate.shape[1:]
    tf = 512
    nb = cap // tm
    cl = lambda i, nu: jnp.minimum(i, nu[0] - 1)
    wspec_in = pl.BlockSpec((None, None, d, tf), lambda i, f, be, nu, nv: (layer, be[cl(i, nu)], 0, f))
    wspec_out = pl.BlockSpec((None, None, tf, d), lambda i, f, be, nu, nv: (layer, be[cl(i, nu)], f, 0))
    bspec_in = pl.BlockSpec((1, 1, tf), lambda i, f, be, nu, nv: (layer * n_e + be[cl(i, nu)], 0, f))
    bspec_out = pl.BlockSpec((1, 1, d), lambda i, f, be, nu, nv: (layer * n_e + be[cl(i, nu)], 0, 0))
    grid_spec = pltpu.PrefetchScalarGridSpec(
        num_scalar_prefetch=3,
        grid=(nb, ff // tf),
        in_specs=[
            pl.BlockSpec((tm, half), lambda i, f, be, nu, nv: (cl(i, nu), 0)),
            wspec_in, bspec_in, wspec_in, bspec_in, wspec_out, bspec_out,
        ],
        out_specs=pl.BlockSpec((tm, d), lambda i, f, be, nu, nv: (i, 0)),
        scratch_shapes=[pltpu.VMEM((tm, d), BF16)],
    )
    return pl.pallas_call(
        _expert_kernel,
        grid_spec=grid_spec,
        out_shape=jax.ShapeDtypeStruct((cap, d), F32),
        compiler_params=_cparams(("parallel", "arbitrary")),
        name="experts",
    )(block_e, n_used, n_valid, xb, w_gate, b_gate.reshape(-1, 1, ff), w_up, b_up.reshape(-1, 1, ff),
      w_down, b_down.reshape(-1, 1, d))


def _combine_kernel(dcur_ref, dnxt_ref, w_ref, x_ref, mod_ref, yb_ref, o_ref, ybuf, sems, *, kg):
    i = pl.program_id(0)
    n = pl.num_programs(0)
    tm = x_ref.shape[0]
    slot = i % 2

    def fetch(d_ref, s):
        def issue(t, carry):
            for k in range(TOP_K):
                pltpu.make_async_copy(yb_ref.at[pl.ds(d_ref[k, t], 1)], ybuf.at[s, k, pl.ds(t, 1)],
                                      sems.at[s]).start()
            return carry

        lax.fori_loop(0, tm, issue, 0, unroll=8)

    @pl.when(i == 0)
    def _():
        fetch(dcur_ref, 0)

    @pl.when(i + 1 < n)
    def _():
        fetch(dnxt_ref, 1 - slot)

    for k in range(TOP_K):
        pltpu.make_async_copy(yb_ref.at[pl.ds(0, tm)], ybuf.at[slot, k], sems.at[slot]).wait()
    w = w_ref[...]
    f = ybuf[slot, 0] * w[:, 0:1]
    for k in range(1, TOP_K):
        f = f + ybuf[slot, k] * w[:, k:k + 1]
    o_ref[...] = x_ref[...] + mod_ref[0][kg:kg + 1, :] * f


def _combine(geo, yb, dest, w_tk, x, mod, kg):
    t, d = x.shape
    tm = _pick(128, geo.t_ctx, geo.seq_lat)
    nsteps = t // tm
    row = geo.mod_row(tm)
    return pl.pallas_call(
        functools.partial(_combine_kernel, kg=kg),
        grid=(nsteps,),
        in_specs=[
            pl.BlockSpec((TOP_K, tm), lambda i: (0, i), memory_space=pltpu.SMEM),
            pl.BlockSpec((TOP_K, tm), lambda i: (0, jnp.minimum(i + 1, nsteps - 1)), memory_space=pltpu.SMEM),
            pl.BlockSpec((tm, TOP_K), lambda i: (i, 0)),
            pl.BlockSpec((tm, d), lambda i: (i, 0)),
            pl.BlockSpec((1, 6, d), lambda i: (row(i), 0, 0)),
            pl.BlockSpec(memory_space=pl.ANY),
        ],
        out_specs=pl.BlockSpec((tm, d), lambda i: (i, 0)),
        out_shape=jax.ShapeDtypeStruct((t, d), F32),
        scratch_shapes=[pltpu.VMEM((2, TOP_K, tm, d), F32), pltpu.SemaphoreType.DMA((2,))],
        compiler_params=_dma_cparams(("arbitrary",)),
        name="combine",
    )(dest, dest, w_tk, x, mod, yb)


MOE_TM = 1024


def _moe_blocks(t, n_e):
    return -(-(t * TOP_K) // MOE_TM) + n_e


def _moe(geo, x, nw, mod, layer, w_router, b_router, w_gate, b_gate, w_up, b_up, w_down, b_down, xb_buf):
    t, d = x.shape
    n_e = w_router.shape[1]
    hp, top_e, top_w, rank, cnt = _router(geo, x, nw, mod, w_router.T, b_router, 3)
    counts = cnt[:, 0].astype(I32)
    padded = (counts + MOE_TM - 1) // MOE_TM * MOE_TM
    pad_end = jnp.cumsum(padded)
    pad_start = pad_end - padded
    nb = _moe_blocks(t, n_e)
    experts = jnp.arange(n_e, dtype=I32)
    start_of = jnp.sum(jnp.where(top_e[None] == experts[:, None, None], pad_start[:, None, None], 0), axis=0)
    dest = start_of + rank
    blk0 = jnp.arange(nb, dtype=I32) * MOE_TM
    block_e = jnp.minimum(jnp.sum((pad_end[None, :] <= blk0[:, None]).astype(I32), axis=1), n_e - 1)
    own = block_e[:, None] == experts[None, :]
    group_end = jnp.sum(jnp.where(own, (pad_start + counts)[None, :], 0), axis=1)
    n_valid = jnp.clip(group_end - blk0, 0, MOE_TM).astype(I32)
    n_used = (pad_end[-1:] // MOE_TM).astype(I32)
    xb = _dispatch(hp, dest, xb_buf)
    yb = _experts(xb, block_e, n_used, n_valid, layer, w_gate, b_gate, w_up, b_up, w_down, b_down, MOE_TM)
    return _combine(geo, yb, dest, top_w.T, x, mod, 5), xb


_SWAP = np.array([(j + ROPE_FREQ) if (j // ROPE_FREQ) % 2 == 0 else (j - ROPE_FREQ) for j in range(QK_ROPE)])


def _rope_table(seq_lat, tq):
    pos = jnp.arange(seq_lat)
    row = (pos // GRID_W).astype(F32)
    colp = (pos % GRID_W).astype(F32)
    inv = jnp.power(ROPE_THETA, -jnp.arange(ROPE_FREQ, dtype=F32) / ROPE_FREQ)
    ar, ac = row[:, None] * inv, colp[:, None] * inv
    cos = jnp.concatenate([jnp.cos(ar), jnp.cos(ar), jnp.cos(ac), jnp.cos(ac)], axis=1)
    sin = jnp.concatenate([-jnp.sin(ar), jnp.sin(ar), -jnp.sin(ac), jnp.sin(ac)], axis=1)
    ident = jnp.concatenate([jnp.ones((tq, QK_ROPE), F32), jnp.zeros((tq, QK_ROPE), F32)], axis=1)
    return jnp.concatenate([ident, jnp.concatenate([cos, sin], axis=1)], axis=0)


def _even_params(w_in, w_qb, q_norm, q_rope_norm, k_rope_norm):
    d = w_in.shape[0]
    o_kv, o_kr, o_conv = Q_LORA, Q_LORA + KV_LORA, Q_LORA + KV_LORA + QK_ROPE
    kr = w_in[:, o_kr:o_conv]
    pad = jnp.zeros((d, EV_N - EV_KR - 2 * QK_ROPE), w_in.dtype)
    w_in_r = jnp.concatenate([w_in[:, o_conv:], w_in[:, :o_kr], kr, kr[:, _SWAP], pad], axis=1).astype(BF16)
    wq = w_qb.reshape(Q_LORA, MLA_HEADS, QK_NOPE + QK_ROPE)
    wq_r = jnp.concatenate([wq, wq[:, :, QK_NOPE:][:, :, _SWAP]], axis=2).reshape(Q_LORA, MLA_HEADS * 256)
    q_nw = jnp.concatenate([q_norm, q_rope_norm, q_rope_norm[_SWAP]]).reshape(1, 256)
    kr_nw = jnp.concatenate([k_rope_norm, k_rope_norm[_SWAP]]).reshape(1, 2 * QK_ROPE)
    return w_in_r, wq_r.astype(BF16), q_nw, kr_nw


def _odd_params(w_in, w_gk, b_gk):
    d = w_in.shape[0]
    pad = jnp.zeros((d, OD_N - w_in.shape[1]), w_in.dtype)
    w_in_r = jnp.concatenate([w_in, pad], axis=1).astype(BF16)
    wpad = jnp.zeros((2, LANES, w_gk.shape[2]), F32)
    wpad = wpad.at[0, :GATE_RANK].set(w_gk[0]).at[1, GATE_RANK:2 * GATE_RANK].set(w_gk[1])
    return w_in_r, wpad.astype(BF16), b_gk.reshape(2, 1, -1)


def kernel(x_prompt, x_sample, cache_mla_ckv, cache_mla_krope, state_gla, c, c_ctx, ada_w, ada_b, norm_mix_w, norm_ffn_w, ev_w_in, ev_q_a_norm, ev_w_qb, ev_kv_a_norm, ev_w_kvb, ev_q_norm, ev_k_norm, ev_q_rope_norm, ev_k_rope_norm, ev_conv_w, ev_w_out, od_w_in, od_w_gk, od_b_gk, od_o_norm, od_w_out, moe_w_router, moe_b_router, moe_w_gate, moe_b_gate, moe_w_up, moe_b_up, moe_w_down, moe_b_down):
    geo = _Geo(x_prompt, x_sample, cache_mla_ckv)
    d = geo.d
    depth = ada_w.shape[0]
    x = jnp.concatenate([x_prompt.reshape(geo.t_ctx, d), x_sample.reshape(geo.t_lat, d)], axis=0)
    cond = jnp.concatenate([c_ctx[None, :], c, jnp.zeros((16 - 1 - geo.n_lat, d), F32)], axis=0)
    mods = _adaln(cond, ada_w, ada_b).reshape(depth, 16, 6, d)
    tq = _pick(512, geo.t_ctx, geo.seq_lat)
    tab = _rope_table(geo.seq_lat, tq)
    w_gate, w_up, w_down = moe_w_gate.astype(BF16), moe_w_up.astype(BF16), moe_w_down.astype(BF16)
    xb_buf = jnp.zeros((_moe_blocks(geo.t, moe_w_router.shape[2]) * MOE_TM, d // 2), U32)
    attn_buf = jnp.zeros((geo.t, MLA_HEADS * V_HEAD), BF16)
    gla_bufs = (jnp.zeros((geo.t, GLA_HEADS * GLA_DVH), BF16), jnp.zeros((geo.t, GLA_HEADS * GLA_DVH), BF16))
    new_ckv, new_krope, new_gla = [], [], []
    for l in range(depth):
        i = l // 2
        mod = mods[l]
        if l % 2 == 0:
            w_in_r, wq_r, q_nw, kr_nw = _even_params(ev_w_in[i], ev_w_qb[i], ev_q_norm[i], ev_q_rope_norm[i],
                                                     ev_k_rope_norm[i])
            z = _lnmod_mm(geo, x, norm_mix_w[l], mod, w_in_r, 0)
            conv, qan, ckv, krn, krot = _even_mid(geo, z, ev_conv_w[i], ev_q_a_norm[i], ev_kv_a_norm[i], kr_nw,
                                                  tab, tq)
            q256 = _q_proj(geo, qan, wq_r, q_nw, tab, tq)
            kc = cache_mla_krope[:, i].reshape(geo.n_lat * geo.past, QK_ROPE)
            ckv_all = jnp.concatenate([ckv, cache_mla_ckv[:, i].reshape(geo.n_lat * geo.past, KV_LORA)], axis=0)
            krot_all = jnp.concatenate([krot, jnp.concatenate([kc, kc], axis=1).astype(BF16)], axis=0)
            k256, v256 = _kv_proj(ckv_all, ev_w_kvb[i].astype(BF16), ev_k_norm[i], krot_all)
            attn_buf = _attention(geo, q256, k256, v256, attn_buf)
            x = _mix_out(geo, attn_buf, conv, ev_w_out[i].astype(BF16), x, mod, 2)
            new_ckv.append(ckv[:geo.t_ctx].reshape(geo.n_ctx, geo.seq_ctx, KV_LORA))
            new_krope.append(krn[:geo.t_ctx].reshape(geo.n_ctx, geo.seq_ctx, QK_ROPE))
        else:
            w_in_r, wgk_pad, bgk = _odd_params(od_w_in[i], od_w_gk[i], od_b_gk[i])
            z = _lnmod_mm(geo, x, norm_mix_w[l], mod, w_in_r, 0)
            o_f, o_b, s_fin = _gla_scan(z, wgk_pad, bgk, geo.n_ctx, geo.seq_ctx, 0, None, True, gla_bufs)
            gla_bufs = _gla_scan(z, wgk_pad, bgk, geo.n_lat, geo.seq_lat, geo.t_ctx, state_gla[:, i], False,
                                 (o_f, o_b))
            x = _gla_out(geo, gla_bufs[0], gla_bufs[1], z, od_o_norm[i], od_w_out[i].astype(BF16), x, mod, 2)
            new_gla.append(s_fin)
        x, xb_buf = _moe(geo, x, norm_ffn_w[l], mod, l, moe_w_router[l], moe_b_router[l], w_gate, moe_b_gate,
                         w_up, moe_b_up, w_down, moe_b_down, xb_buf)
    xp = x[:geo.t_ctx].reshape(x_prompt.shape)
    xs = x[geo.t_ctx:].reshape(x_sample.shape)
    return (xp, xs, jnp.stack(new_ckv, axis=1), jnp.stack(new_krope, axis=1), jnp.stack(new_gla, axis=1))
```

```python
import functools

import jax
import jax.numpy as jnp
import numpy as np
from jax import lax
from jax.experimental import pallas as pl
from jax.experimental.pallas import tpu as pltpu

F32 = jnp.float32
BF16 = jnp.bfloat16
I32 = jnp.int32
U32 = jnp.uint32

EPS = 1e-6
GRID_W = 64
MLA_HEADS = 8
QK_NOPE = 128
QK_ROPE = 64
V_HEAD = 128
Q_LORA = 512
KV_LORA = 256
ROPE_THETA = 10000.0
ROPE_FREQ = QK_ROPE // 4
CONV_W = 1024
GLA_HEADS = 4
GLA_DKH = 256
GLA_DVH = 512
GATE_RANK = 16
GATE_NORM = 16.0
GLA_CHUNK = 64
TOP_K = 4
SWIGLU_ALPHA = 1.702
SWIGLU_LIMIT = 7.0

V7X_VMEM_LIMIT = 56 * 1024 * 1024
LANES = 128
HI = lax.Precision.HIGHEST


def _cparams(sem):
    return pltpu.CompilerParams(dimension_semantics=sem, vmem_limit_bytes=V7X_VMEM_LIMIT)


def _pick(want, *dims):
    t = want
    while any(d % t for d in dims):
        t //= 2
    return t


def _dot(a, b):
    return jnp.dot(a, b, preferred_element_type=F32)


def _rms(x, w):
    return x * lax.rsqrt(jnp.mean(x * x, axis=-1, keepdims=True) + EPS) * w


def _lnmod(x, nw, mod, ks):
    gain = nw * (1.0 + mod[ks + 1:ks + 2, :])
    return x * lax.rsqrt(jnp.mean(x * x, axis=-1, keepdims=True) + EPS) * gain + mod[ks:ks + 1, :]


class _Geo:
    def __init__(self, x_prompt, x_sample, cache_ckv):
        self.n_ctx, self.seq_ctx, self.d = x_prompt.shape
        self.n_lat, self.seq_lat, _ = x_sample.shape
        self.past = cache_ckv.shape[2]
        self.t_ctx = self.n_ctx * self.seq_ctx
        self.t_lat = self.n_lat * self.seq_lat
        self.t = self.t_ctx + self.t_lat

    def mod_row(self, tm):
        t_ctx, seq_lat = self.t_ctx, self.seq_lat
        return lambda i: jnp.where(i * tm < t_ctx, 0, 1 + (i * tm - t_ctx) // seq_lat)


def _adaln_kernel(c_ref, w_ref, b_ref, o_ref):
    c = c_ref[...]
    a = (c * jax.nn.sigmoid(c)).astype(BF16)
    o_ref[0] = _dot(a, w_ref[0].astype(BF16)) + b_ref[0]


def _adaln(cond16, ada_w, ada_b):
    depth, d, n = ada_w.shape
    tn = 1024
    return pl.pallas_call(
        _adaln_kernel,
        grid=(depth, n // tn),
        in_specs=[
            pl.BlockSpec((16, d), lambda l, j: (0, 0)),
            pl.BlockSpec((1, d, tn), lambda l, j: (l, 0, j)),
            pl.BlockSpec((1, 1, tn), lambda l, j: (l, 0, j)),
        ],
        out_specs=pl.BlockSpec((1, 16, tn), lambda l, j: (l, 0, j)),
        out_shape=jax.ShapeDtypeStruct((depth, 16, n), F32),
        compiler_params=_cparams(("parallel", "parallel")),
        name="adaln",
    )(cond16, ada_w, ada_b.reshape(depth, 1, n))


def _lnmod_mm_kernel(x_ref, nw_ref, mod_ref, w_ref, o_ref, h_scr, *, ks):
    @pl.when(pl.program_id(1) == 0)
    def _():
        h_scr[...] = _lnmod(x_ref[...], nw_ref[...], mod_ref[0], ks).astype(BF16)

    o_ref[...] = _dot(h_scr[...], w_ref[...])


def _lnmod_mm(geo, x, nw, mod, w, ks):
    t, d = x.shape
    n = w.shape[1]
    tm = _pick(1024, geo.t_ctx, geo.seq_lat)
    tn = max(c for c in range(LANES, 1024 + 1, LANES) if n % c == 0)
    row = geo.mod_row(tm)
    return pl.pallas_call(
        functools.partial(_lnmod_mm_kernel, ks=ks),
        grid=(t // tm, n // tn),
        in_specs=[
            pl.BlockSpec((tm, d), lambda i, j: (i, 0)),
            pl.BlockSpec((1, d), lambda i, j: (0, 0)),
            pl.BlockSpec((1, 6, d), lambda i, j: (row(i), 0, 0)),
            pl.BlockSpec((d, tn), lambda i, j: (0, j)),
        ],
        out_specs=pl.BlockSpec((tm, tn), lambda i, j: (i, j)),
        out_shape=jax.ShapeDtypeStruct((t, n), F32),
        scratch_shapes=[pltpu.VMEM((tm, d), BF16)],
        compiler_params=_cparams(("parallel", "arbitrary")),
        name="lnmod_mm",
    )(x, nw.reshape(1, d), mod, w)


EV_QA = 3 * CONV_W
EV_KVA = EV_QA + Q_LORA
EV_KR = EV_KVA + KV_LORA
EV_N = 4096


def _even_mid_kernel(bg_ref, cg_ref, xv_ref, cgp_ref, xvp_ref, cgn_ref, xvn_ref, qa_ref, kva_ref, kr_ref,
                     cw_ref, qaw_ref, kvw_ref, krw_ref, tab_ref,
                     conv_ref, qan_ref, ckv_ref, krn_ref, krot_ref, *, t_ctx, seq_ctx, seq_lat):
    i = pl.program_id(0)
    tm = bg_ref.shape[0]
    u = cg_ref[...] * xv_ref[...]
    rows = lax.broadcasted_iota(I32, (tm, 1), 0)
    tok = i * tm + rows
    pos = jnp.where(tok < t_ctx, tok & (seq_ctx - 1), (tok - t_ctx) & (seq_lat - 1))
    last = jnp.where(tok < t_ctx, seq_ctx - 1, seq_lat - 1)
    u_m1 = jnp.where(rows == 0, cgp_ref[7:8, :] * xvp_ref[7:8, :], pltpu.roll(u, 1, axis=0))
    u_p1 = jnp.where(rows == tm - 1, cgn_ref[0:1, :] * xvn_ref[0:1, :], pltpu.roll(u, tm - 1, axis=0))
    u_m1 = jnp.where(pos == 0, 0.0, u_m1)
    u_p1 = jnp.where(pos == last, 0.0, u_p1)
    cw = cw_ref[...]
    conv = bg_ref[...] * (u_m1 * cw[0:1, :] + u * cw[1:2, :] + u_p1 * cw[2:3, :])
    conv_ref[...] = conv.astype(BF16)
    qan_ref[...] = _rms(qa_ref[...], qaw_ref[...]).astype(BF16)
    ckv_ref[...] = _rms(kva_ref[...], kvw_ref[...])
    kr = kr_ref[...]
    krn = _rms(kr, krw_ref[...])
    krn_ref[...] = krn[:, :QK_ROPE]
    y = krn * tab_ref[...]
    krot_ref[...] = (y + pltpu.roll(y, QK_ROPE, axis=1)).astype(BF16)


def _even_mid(geo, z, conv_w, qa_w, kv_w, kr_w128, tab, tq):
    t = z.shape[0]
    tm = tq
    ncb = geo.t_ctx // tm
    bps = geo.seq_lat // tm
    nb8 = t // 8
    r8 = tm // 8
    assert geo.seq_ctx & (geo.seq_ctx - 1) == 0 and geo.seq_lat & (geo.seq_lat - 1) == 0
    tab_idx = lambda i: jnp.where(i < ncb, 0, 1 + (i - ncb) % bps)
    col = lambda c: (lambda i: (i, c))
    in_specs = [
        pl.BlockSpec((tm, CONV_W), col(0)),
        pl.BlockSpec((tm, CONV_W), col(1)),
        pl.BlockSpec((tm, CONV_W), col(2)),
        pl.BlockSpec((8, CONV_W), lambda i: (jnp.maximum(i * r8 - 1, 0), 1)),
        pl.BlockSpec((8, CONV_W), lambda i: (jnp.maximum(i * r8 - 1, 0), 2)),
        pl.BlockSpec((8, CONV_W), lambda i: (jnp.minimum((i + 1) * r8, nb8 - 1), 1)),
        pl.BlockSpec((8, CONV_W), lambda i: (jnp.minimum((i + 1) * r8, nb8 - 1), 2)),
        pl.BlockSpec((tm, Q_LORA), col(EV_QA // Q_LORA)),
        pl.BlockSpec((tm, KV_LORA), col(EV_KVA // KV_LORA)),
        pl.BlockSpec((tm, 2 * QK_ROPE), col(EV_KR // (2 * QK_ROPE))),
        pl.BlockSpec((3, CONV_W), lambda i: (0, 0)),
        pl.BlockSpec((1, Q_LORA), lambda i: (0, 0)),
        pl.BlockSpec((1, KV_LORA), lambda i: (0, 0)),
        pl.BlockSpec((1, 2 * QK_ROPE), lambda i: (0, 0)),
        pl.BlockSpec((tm, 2 * QK_ROPE), lambda i: (tab_idx(i), 0)),
    ]
    out_specs = [
        pl.BlockSpec((tm, CONV_W), lambda i: (i, 0)),
        pl.BlockSpec((tm, Q_LORA), lambda i: (i, 0)),
        pl.BlockSpec((tm, KV_LORA), lambda i: (i, 0)),
        pl.BlockSpec((tm, QK_ROPE), lambda i: (i, 0)),
        pl.BlockSpec((tm, 2 * QK_ROPE), lambda i: (i, 0)),
    ]
    out_shape = [
        jax.ShapeDtypeStruct((t, CONV_W), BF16),
        jax.ShapeDtypeStruct((t, Q_LORA), BF16),
        jax.ShapeDtypeStruct((t, KV_LORA), F32),
        jax.ShapeDtypeStruct((t, QK_ROPE), F32),
        jax.ShapeDtypeStruct((t, 2 * QK_ROPE), BF16),
    ]
    return pl.pallas_call(
        functools.partial(_even_mid_kernel, t_ctx=geo.t_ctx, seq_ctx=geo.seq_ctx, seq_lat=geo.seq_lat),
        grid=(t // tm,),
        in_specs=in_specs,
        out_specs=out_specs,
        out_shape=out_shape,
        compiler_params=_cparams(("parallel",)),
        name="even_mid",
    )(z, z, z, z, z, z, z, z, z, z, conv_w, qa_w.reshape(1, -1), kv_w.reshape(1, -1), kr_w128, tab)


def _q_proj_kernel(a_ref, w_ref, nw_ref, tab_ref, o_ref):
    acc = _dot(a_ref[...], w_ref[...])
    nw = nw_ref[...]
    tab = tab_ref[...]
    for h in range(MLA_HEADS):
        n = acc[:, h * 256:h * 256 + QK_NOPE]
        r = acc[:, h * 256 + QK_NOPE:(h + 1) * 256]
        qn = _rms(n, nw[:, :QK_NOPE])
        y = _rms(r, nw[:, QK_NOPE:]) * tab
        o_ref[h] = (jnp.concatenate([qn, y], axis=-1) * Q_PRESCALE).astype(BF16)


def _q_proj(geo, qan, w_qb, nw256, tab, tq):
    t = qan.shape[0]
    tm = tq
    ncb = geo.t_ctx // tm
    bps = geo.seq_lat // tm
    n = w_qb.shape[1]
    return pl.pallas_call(
        _q_proj_kernel,
        grid=(t // tm,),
        in_specs=[
            pl.BlockSpec((tm, Q_LORA), lambda i: (i, 0)),
            pl.BlockSpec((Q_LORA, n), lambda i: (0, 0)),
            pl.BlockSpec((1, 256), lambda i: (0, 0)),
            pl.BlockSpec((tm, 2 * QK_ROPE), lambda i: (jnp.where(i < ncb, 0, 1 + (i - ncb) % bps), 0)),
        ],
        out_specs=pl.BlockSpec((MLA_HEADS, tm, 256), lambda i: (0, i, 0)),
        out_shape=jax.ShapeDtypeStruct((MLA_HEADS, t, 256), BF16),
        compiler_params=_cparams(("parallel",)),
        name="q_proj",
    )(qan, w_qb, nw256, tab)


def _kv_proj_kernel(a_ref, w_ref, nw_ref, kr_ref, k_ref, v_ref):
    acc = _dot(a_ref[...].astype(BF16), w_ref[...])
    kr = kr_ref[...]
    ones = jnp.ones((acc.shape[0], V_HEAD), BF16)
    for h in range(MLA_HEADS):
        kn = _rms(acc[:, h * 256:h * 256 + QK_NOPE], nw_ref[...]).astype(BF16)
        k_ref[h] = jnp.concatenate([kn, kr], axis=-1)
        v_ref[h] = jnp.concatenate([acc[:, h * 256 + QK_NOPE:(h + 1) * 256].astype(BF16), ones], axis=-1)


def _kv_proj(ckv_all, w_kvb, k_norm, krot_all):
    t = ckv_all.shape[0]
    tm = _pick(512, t)
    n = w_kvb.shape[1]
    return pl.pallas_call(
        _kv_proj_kernel,
        grid=(t // tm,),
        in_specs=[
            pl.BlockSpec((tm, KV_LORA), lambda i: (i, 0)),
            pl.BlockSpec((KV_LORA, n), lambda i: (0, 0)),
            pl.BlockSpec((1, QK_NOPE), lambda i: (0, 0)),
            pl.BlockSpec((tm, 2 * QK_ROPE), lambda i: (i, 0)),
        ],
        out_specs=[
            pl.BlockSpec((MLA_HEADS, tm, 256), lambda i: (0, i, 0)),
            pl.BlockSpec((MLA_HEADS, tm, 2 * V_HEAD), lambda i: (0, i, 0)),
        ],
        out_shape=[
            jax.ShapeDtypeStruct((MLA_HEADS, t, 256), BF16),
            jax.ShapeDtypeStruct((MLA_HEADS, t, 2 * V_HEAD), BF16),
        ],
        compiler_params=_cparams(("parallel",)),
        name="kv_proj",
    )(ckv_all, w_kvb, k_norm.reshape(1, -1), krot_all)


_NT = (((1,), (1,)), ((), ()))
_TN = (((0,), (0,)), ((), ()))
Q_PRESCALE = (QK_NOPE + QK_ROPE) ** -0.5 * 1.4426950408889634
ATTN_KEY_CHUNK = 512


def _softmax_step(q, k, v, m, acc):
    s = lax.dot_general(q, k, _NT, preferred_element_type=F32)
    m_new = jnp.maximum(m, jnp.max(s, axis=-1, keepdims=True))
    p = jnp.exp2(s - m_new)
    acc = acc * jnp.exp2(m - m_new) + _dot(p.astype(BF16), v)
    return m_new, acc


def _attn_ctx_kernel(q_ref, k_ref, v_ref, prev_ref, o_ref):
    del prev_ref
    outs = []
    for h in range(MLA_HEADS):
        s = lax.dot_general(q_ref[h], k_ref[h], _NT, preferred_element_type=F32)
        p = jnp.exp2(s - jnp.max(s, axis=-1, keepdims=True))
        o = _dot(p.astype(BF16), v_ref[h])
        outs.append(o[:, :V_HEAD] / o[:, V_HEAD:V_HEAD + 1])
    o_ref[...] = jnp.concatenate(outs, axis=-1).astype(BF16)


def _attn_lat_kernel(q_ref, kc_ref, vc_ref, kl_ref, vl_ref, prev_ref, o_ref):
    del prev_ref
    q = q_ref[0]
    tq = q.shape[0]
    m = jnp.full((tq, 1), -jnp.inf, F32)
    acc = jnp.zeros((tq, 2 * V_HEAD), F32)
    for k_ref, v_ref in ((kc_ref, vc_ref), (kl_ref, vl_ref)):
        n = k_ref.shape[1]
        ck = min(ATTN_KEY_CHUNK, n)
        for j in range(n // ck):
            m, acc = _softmax_step(q, k_ref[0, j * ck:(j + 1) * ck, :], v_ref[0, j * ck:(j + 1) * ck, :], m, acc)
    o_ref[...] = (acc[:, :V_HEAD] / acc[:, V_HEAD:V_HEAD + 1]).astype(BF16)


def _attention(geo, q256, k256, v256, o_buf):
    t = geo.t
    sc, sl, past = geo.seq_ctx, geo.seq_lat, geo.past
    hv = MLA_HEADS * V_HEAD
    vw = 2 * V_HEAD
    o_ctx = pl.pallas_call(
        _attn_ctx_kernel,
        grid=(geo.n_ctx,),
        in_specs=[
            pl.BlockSpec((MLA_HEADS, sc, 256), lambda b: (0, b, 0)),
            pl.BlockSpec((MLA_HEADS, sc, 256), lambda b: (0, b, 0)),
            pl.BlockSpec((MLA_HEADS, sc, vw), lambda b: (0, b, 0)),
            pl.BlockSpec(memory_space=pl.ANY),
        ],
        out_specs=pl.BlockSpec((sc, hv), lambda b: (b, 0)),
        out_shape=jax.ShapeDtypeStruct((t, hv), BF16),
        input_output_aliases={3: 0},
        compiler_params=_cparams(("parallel",)),
        name="attn_ctx",
    )(q256, k256, v256, o_buf)
    tq = _pick(512, sl)
    nq = sl // tq
    q_off = geo.t_ctx // tq
    kl_off = geo.t_ctx // sl
    kc_off = t // past
    return pl.pallas_call(
        _attn_lat_kernel,
        grid=(geo.n_lat, MLA_HEADS, nq),
        in_specs=[
            pl.BlockSpec((1, tq, 256), lambda b, h, i: (h, q_off + b * nq + i, 0)),
            pl.BlockSpec((1, past, 256), lambda b, h, i: (h, kc_off + b, 0)),
            pl.BlockSpec((1, past, vw), lambda b, h, i: (h, kc_off + b, 0)),
            pl.BlockSpec((1, sl, 256), lambda b, h, i: (h, kl_off + b, 0)),
            pl.BlockSpec((1, sl, vw), lambda b, h, i: (h, kl_off + b, 0)),
            pl.BlockSpec(memory_space=pl.ANY),
        ],
        out_specs=pl.BlockSpec((tq, V_HEAD), lambda b, h, i: (q_off + b * nq + i, h)),
        out_shape=jax.ShapeDtypeStruct((t, hv), BF16),
        input_output_aliases={5: 0},
        compiler_params=_cparams(("parallel", "parallel", "arbitrary")),
        name="attn_lat",
    )(q256, k256, v256, k256, v256, o_ctx)


def _mix_out_kernel(a1_ref, a2_ref, w_ref, x_ref, mod_ref, o_ref, *, kg):
    k1 = a1_ref.shape[1]
    acc = _dot(a1_ref[...], w_ref[:k1, :]) + _dot(a2_ref[...], w_ref[k1:, :])
    o_ref[...] = x_ref[...] + mod_ref[0][kg:kg + 1, :] * acc


def _mix_out(geo, a1, a2, w, x, mod, kg):
    t, d = x.shape
    tm = _pick(512, geo.t_ctx, geo.seq_lat)
    row = geo.mod_row(tm)
    k1, k2 = a1.shape[1], a2.shape[1]
    return pl.pallas_call(
        functools.partial(_mix_out_kernel, kg=kg),
        grid=(t // tm,),
        in_specs=[
            pl.BlockSpec((tm, k1), lambda i: (i, 0)),
            pl.BlockSpec((tm, k2), lambda i: (i, 0)),
            pl.BlockSpec((k1 + k2, d), lambda i: (0, 0)),
            pl.BlockSpec((tm, d), lambda i: (i, 0)),
            pl.BlockSpec((1, 6, d), lambda i: (row(i), 0, 0)),
        ],
        out_specs=pl.BlockSpec((tm, d), lambda i: (i, 0)),
        out_shape=jax.ShapeDtypeStruct((t, d), F32),
        compiler_params=_cparams(("parallel",)),
        name="mix_out",
    )(a1, a2, w, x, mod)


OD_K = GLA_HEADS * GLA_DKH
OD_V = 2 * OD_K
OD_G = OD_V + GLA_HEADS * GLA_DVH
OD_R = OD_G + GLA_HEADS * GLA_DVH
OD_N = OD_R + LANES


def _log_sigmoid(x):
    return jnp.minimum(x, 0.0) - jnp.log1p(jnp.exp(-jnp.abs(x)))


def _split3(x):
    hi = x.astype(BF16)
    r1 = x - hi.astype(F32)
    mid = r1.astype(BF16)
    lo = (r1 - mid.astype(F32)).astype(BF16)
    return hi, mid, lo


def _gla_chunk(q, k, v, r, wgk, bgk, st_ref, o_ref, backward):
    c = q.shape[0]
    la = _log_sigmoid(_dot(r.astype(BF16), wgk) + bgk) * (1.0 / GATE_NORM)
    ii = lax.broadcasted_iota(I32, (c, c), 0)
    jj = lax.broadcasted_iota(I32, (c, c), 1)
    causal = (ii <= jj) if backward else (ii >= jj)
    tri = causal.astype(BF16)
    hi, mid, lo = _split3(la)
    b = _dot(tri, hi) + _dot(tri, mid) + _dot(tri, lo)
    tot = jnp.sum(la, axis=0, keepdims=True)
    q_dec = (q * (GLA_DKH ** -0.5) * jnp.exp(b)).astype(BF16)
    k_dec = (k * jnp.exp(-b)).astype(BF16)
    k_end = (k * jnp.exp(tot - b)).astype(BF16)
    e_tot = jnp.exp(tot)
    vb = v.astype(BF16)
    heads = range(GLA_HEADS)
    ks = [slice(h * GLA_DKH, (h + 1) * GLA_DKH) for h in heads]
    vs = [slice(h * GLA_DVH, (h + 1) * GLA_DVH) for h in heads]
    scores = [jnp.where(causal, lax.dot_general(q_dec[:, ks[h]], k_dec[:, ks[h]], _NT, preferred_element_type=F32),
                        0.0).astype(BF16) for h in heads]
    st = [st_ref[h] for h in heads]
    for h in heads:
        o_ref[:, vs[h]] = (_dot(scores[h], vb[:, vs[h]]) + lax.dot_general(
            q_dec[:, ks[h]], st[h].astype(BF16), _NT, preferred_element_type=F32)).astype(o_ref.dtype)
    for h in heads:
        st_ref[h] = e_tot[:, ks[h]] * st[h] + lax.dot_general(vb[:, vs[h]], k_end[:, ks[h]], _TN,
                                                             preferred_element_type=F32)


def _gla_kernel(*refs, has_init, want_final):
    qf, kf, vf, rf, qb, kb, vb, rb, wgk_ref, bgk_ref = refs[:10]
    pos = 10
    s0_ref = None
    if has_init:
        s0_ref = refs[pos]
        pos += 1
    pos += 2
    of_ref, ob_ref = refs[pos], refs[pos + 1]
    pos += 2
    sfin_ref = None
    if want_final:
        sfin_ref = refs[pos]
        pos += 1
    st_scr = refs[pos]
    c = pl.program_id(1)
    nc = pl.num_programs(1)

    @pl.when(c == 0)
    def _():
        for d in range(2):
            for h in range(GLA_HEADS):
                if has_init:
                    st_scr[d, h] = s0_ref[0, d, h].T
                else:
                    st_scr[d, h] = jnp.zeros(st_scr.shape[2:], F32)

    _gla_chunk(qf[...], kf[...], vf[...], rf[...], wgk_ref[0], bgk_ref[0], st_scr.at[0], of_ref, False)
    _gla_chunk(qb[...], kb[...], vb[...], rb[...], wgk_ref[1], bgk_ref[1], st_scr.at[1], ob_ref, True)

    if want_final:
        @pl.when(c == nc - 1)
        def _():
            for d in range(2):
                for h in range(GLA_HEADS):
                    sfin_ref[0, d, h] = st_scr[d, h].T


def _gla_scan(z, wgk_pad, bgk, n_seq, seq, row0, s0, want_final, bufs):
    t = z.shape[0]
    c = GLA_CHUNK
    nc = seq // c
    base = row0 // c
    fwd = lambda b, i: base + b * nc + i
    bwd = lambda b, i: base + b * nc + (nc - 1 - i)
    dk, dv = GLA_HEADS * GLA_DKH, GLA_HEADS * GLA_DVH

    def zspecs(rowf):
        return [
            pl.BlockSpec((c, dk), lambda b, i: (rowf(b, i), 0)),
            pl.BlockSpec((c, dk), lambda b, i: (rowf(b, i), OD_K // dk)),
            pl.BlockSpec((c, dv), lambda b, i: (rowf(b, i), OD_V // dv)),
            pl.BlockSpec((c, LANES), lambda b, i: (rowf(b, i), OD_R // LANES)),
        ]

    in_specs = zspecs(fwd) + zspecs(bwd) + [
        pl.BlockSpec((2, LANES, dk), lambda b, i: (0, 0, 0)),
        pl.BlockSpec((2, 1, dk), lambda b, i: (0, 0, 0)),
    ]
    args = [z] * 8 + [wgk_pad, bgk]
    if s0 is not None:
        in_specs.append(pl.BlockSpec((1, 2, GLA_HEADS, GLA_DKH, GLA_DVH), lambda b, i: (b, 0, 0, 0, 0)))
        args.append(s0)
    aliases = {len(args): 0, len(args) + 1: 1}
    in_specs += [pl.BlockSpec(memory_space=pl.ANY)] * 2
    args += list(bufs)
    out_specs = [pl.BlockSpec((c, dv), lambda b, i: (fwd(b, i), 0)), pl.BlockSpec((c, dv), lambda b, i: (bwd(b, i), 0))]
    out_shape = [jax.ShapeDtypeStruct((t, dv), bufs[0].dtype)] * 2
    if want_final:
        out_specs.append(pl.BlockSpec((1, 2, GLA_HEADS, GLA_DKH, GLA_DVH), lambda b, i: (b, 0, 0, 0, 0)))
        out_shape.append(jax.ShapeDtypeStruct((n_seq, 2, GLA_HEADS, GLA_DKH, GLA_DVH), F32))
    return pl.pallas_call(
        functools.partial(_gla_kernel, has_init=s0 is not None, want_final=want_final),
        grid=(n_seq, nc),
        in_specs=in_specs,
        out_specs=out_specs,
        out_shape=out_shape,
        input_output_aliases=aliases,
        scratch_shapes=[pltpu.VMEM((2, GLA_HEADS, GLA_DVH, GLA_DKH), F32)],
        compiler_params=_cparams(("parallel", "arbitrary")),
        name="gla_scan",
    )(*args)


def _gla_out_kernel(of_ref, ob_ref, g_ref, nw_ref, w_ref, x_ref, mod_ref, o_ref, a_scr, *, kg):
    @pl.when(pl.program_id(1) == 0)
    def _():
        nw = nw_ref[...]
        for h in range(GLA_HEADS):
            sl = slice(h * GLA_DVH, (h + 1) * GLA_DVH)
            o = _rms(of_ref[:, sl].astype(F32) + ob_ref[:, sl].astype(F32), nw)
            g = g_ref[:, sl]
            a_scr[:, sl] = (o * (g * jax.nn.sigmoid(g))).astype(BF16)

    o_ref[...] = x_ref[...] + mod_ref[0][kg:kg + 1, :] * _dot(a_scr[...], w_ref[...])


def _gla_out(geo, o_f, o_b, z, o_norm, w_out, x, mod, kg):
    t, d = x.shape
    dv = GLA_HEADS * GLA_DVH
    tm = _pick(512, geo.t_ctx, geo.seq_lat)
    tn = d
    row = geo.mod_row(tm)
    return pl.pallas_call(
        functools.partial(_gla_out_kernel, kg=kg),
        grid=(t // tm, d // tn),
        in_specs=[
            pl.BlockSpec((tm, dv), lambda i, j: (i, 0)),
            pl.BlockSpec((tm, dv), lambda i, j: (i, 0)),
            pl.BlockSpec((tm, dv), lambda i, j: (i, OD_G // dv)),
            pl.BlockSpec((1, GLA_DVH), lambda i, j: (0, 0)),
            pl.BlockSpec((dv, tn), lambda i, j: (0, j)),
            pl.BlockSpec((tm, tn), lambda i, j: (i, j)),
            pl.BlockSpec((1, 6, tn), lambda i, j: (row(i), 0, j)),
        ],
        out_specs=pl.BlockSpec((tm, tn), lambda i, j: (i, j)),
        out_shape=jax.ShapeDtypeStruct((t, d), F32),
        scratch_shapes=[pltpu.VMEM((tm, dv), BF16)],
        compiler_params=_cparams(("parallel", "arbitrary")),
        name="gla_out",
    )(o_f, o_b, z, o_norm.reshape(1, -1), w_out, x, mod)


def _pack_bf16_pairs(h):
    n = h.shape[1] // 2
    lo = pltpu.bitcast(h[:, :n].astype(BF16).astype(F32), U32)
    hi = pltpu.bitcast(h[:, n:].astype(BF16).astype(F32), U32)
    return (lo >> 16) | (hi & jnp.uint32(0xFFFF0000))


def _unpack_bf16_pairs(w):
    lo = pltpu.bitcast(w << 16, F32).astype(BF16)
    hi = pltpu.bitcast(w & jnp.uint32(0xFFFF0000), F32).astype(BF16)
    return lo, hi


def _router_kernel(x_ref, nw_ref, mod_ref, wr_ref, br_ref, tri_ref,
                   hp_ref, e_ref, w_ref, rank_ref, cnt_ref, carry_scr, *, ks):
    @pl.when(pl.program_id(0) == 0)
    def _():
        carry_scr[...] = jnp.zeros(carry_scr.shape, F32)

    h = _lnmod(x_ref[...], nw_ref[...], mod_ref[0], ks)
    hp_ref[...] = _pack_bf16_pairs(h)
    lg = lax.dot_general(wr_ref[...], h, _NT, precision=HI, preferred_element_type=F32) + br_ref[...]
    n_e, tm = lg.shape
    rows = lax.broadcasted_iota(I32, (n_e, tm), 0).astype(F32)
    vals, sels, hits = [], [], []
    for _ in range(TOP_K):
        m = jnp.max(lg, axis=0, keepdims=True)
        idx = jnp.min(jnp.where(lg == m, rows, float(n_e)), axis=0, keepdims=True)
        hit = rows == idx
        vals.append(m)
        sels.append(idx)
        hits.append(hit)
        lg = jnp.where(hit, -jnp.inf, lg)
    ex = [jnp.exp(v - vals[0]) for v in vals]
    den = ex[0] + ex[1] + ex[2] + ex[3]
    w_ref[...] = jnp.concatenate([e / den for e in ex], axis=0)
    e_ref[...] = jnp.concatenate(sels, axis=0).astype(I32)
    onehot = jnp.zeros((n_e, tm), F32)
    for hit in hits:
        onehot = jnp.where(hit, 1.0, onehot)
    before = carry_scr[:, 0:1] + _dot(onehot.astype(BF16), tri_ref[...])
    rank_ref[...] = jnp.concatenate(
        [jnp.sum(jnp.where(hit, before, 0.0), axis=0, keepdims=True) for hit in hits], axis=0).astype(I32)
    carry_scr[...] = carry_scr[...] + jnp.sum(onehot, axis=1, keepdims=True)
    cnt_ref[...] = carry_scr[...]


def _router(geo, x, nw, mod, w_router_t, b_router, ks):
    t, d = x.shape
    n_e = w_router_t.shape[0]
    tm = _pick(512, geo.t_ctx, geo.seq_lat)
    row = geo.mod_row(tm)
    tri = (lax.broadcasted_iota(I32, (tm, tm), 0) < lax.broadcasted_iota(I32, (tm, tm), 1)).astype(BF16)
    return pl.pallas_call(
        functools.partial(_router_kernel, ks=ks),
        grid=(t // tm,),
        in_specs=[
            pl.BlockSpec((tm, d), lambda i: (i, 0)),
            pl.BlockSpec((1, d), lambda i: (0, 0)),
            pl.BlockSpec((1, 6, d), lambda i: (row(i), 0, 0)),
            pl.BlockSpec((n_e, d), lambda i: (0, 0)),
            pl.BlockSpec((n_e, 1), lambda i: (0, 0)),
            pl.BlockSpec((tm, tm), lambda i: (0, 0)),
        ],
        out_specs=[
            pl.BlockSpec((tm, d // 2), lambda i: (i, 0)),
            pl.BlockSpec((TOP_K, tm), lambda i: (0, i)),
            pl.BlockSpec((TOP_K, tm), lambda i: (0, i)),
            pl.BlockSpec((TOP_K, tm), lambda i: (0, i)),
            pl.BlockSpec((n_e, LANES), lambda i: (0, 0)),
        ],
        out_shape=[
            jax.ShapeDtypeStruct((t, d // 2), U32),
            jax.ShapeDtypeStruct((TOP_K, t), I32),
            jax.ShapeDtypeStruct((TOP_K, t), F32),
            jax.ShapeDtypeStruct((TOP_K, t), I32),
            jax.ShapeDtypeStruct((n_e, LANES), F32),
        ],
        scratch_shapes=[pltpu.VMEM((n_e, LANES), F32)],
        compiler_params=_cparams(("arbitrary",)),
        name="router",
    )(x, nw.reshape(1, d), mod, w_router_t, b_router.reshape(n_e, 1), tri)


def _dma_cparams(sem):
    return pltpu.CompilerParams(dimension_semantics=sem, vmem_limit_bytes=V7X_VMEM_LIMIT,
                                disable_bounds_checks=True)


def _dispatch_kernel(dest_ref, hp_ref, buf_ref, xb_ref, sem):
    del buf_ref
    tm = hp_ref.shape[0]

    for t in range(tm):
        for k in range(TOP_K):
            pltpu.make_async_copy(hp_ref.at[pl.ds(t, 1)], xb_ref.at[pl.ds(dest_ref[k, t], 1)], sem).start()
    for k in range(TOP_K):
        pltpu.make_async_copy(hp_ref, xb_ref.at[pl.ds(0, tm)], sem).wait()


def _dispatch(hp, dest, xb_buf):
    t, half = hp.shape
    tm = _pick(256, t)
    return pl.pallas_call(
        _dispatch_kernel,
        grid=(t // tm,),
        in_specs=[
            pl.BlockSpec((TOP_K, tm), lambda i: (0, i), memory_space=pltpu.SMEM),
            pl.BlockSpec((tm, half), lambda i: (i, 0)),
            pl.BlockSpec(memory_space=pl.ANY),
        ],
        out_specs=pl.BlockSpec(memory_space=pl.ANY),
        out_shape=jax.ShapeDtypeStruct(xb_buf.shape, xb_buf.dtype),
        input_output_aliases={2: 0},
        scratch_shapes=[pltpu.SemaphoreType.DMA],
        compiler_params=_dma_cparams(("arbitrary",)),
        name="dispatch",
    )(dest, hp, xb_buf)


def _expert_kernel(be_ref, nu_ref, nv_ref, xp_ref, wg_ref, bg_ref, wu_ref, bu_ref, wd_ref, bd_ref, o_ref, x_scr):
    i = pl.program_id(0)
    f = pl.program_id(1)
    half = xp_ref.shape[1]

    @pl.when(i >= nu_ref[0])
    def _():
        @pl.when(f == 0)
        def _():
            o_ref[...] = jnp.zeros(o_ref.shape, F32)

    @pl.when(i < nu_ref[0])
    def _():
        @pl.when(f == 0)
        def _():
            keep = lax.broadcasted_iota(I32, (xp_ref.shape[0], 1), 0) < nv_ref[i]
            lo, hi = _unpack_bf16_pairs(jnp.where(keep, xp_ref[...], jnp.uint32(0)))
            x_scr[:, :half] = lo
            x_scr[:, half:] = hi
            o_ref[...] = jnp.broadcast_to(bd_ref[0], o_ref.shape)

        x = x_scr[...]
        gate = jnp.minimum(_dot(x, wg_ref[...]) + bg_ref[0], SWIGLU_LIMIT)
        up = jnp.clip(_dot(x, wu_ref[...]) + bu_ref[0], -SWIGLU_LIMIT, SWIGLU_LIMIT)
        hdn = (up + 1.0) * (gate * jax.nn.sigmoid(SWIGLU_ALPHA * gate))
        o_ref[...] += _dot(hdn.astype(BF16), wd_ref[...])


def _experts(xb, block_e, n_used, n_valid, layer, w_gate, b_gate, w_up, b_up, w_down, b_down, tm):
    cap, half = xb.shape
    d = 2 * half
    n_e, _, ff = w_gate.shape[1:]
    tf = 512
    nb = cap // tm
    cl = lambda i, nu: jnp.minimum(i, nu[0] - 1)
    wspec_in = pl.BlockSpec((None, None, d, tf), lambda i, f, be, nu, nv: (layer, be[cl(i, nu)], 0, f))
    wspec_out = pl.BlockSpec((None, None, tf, d), lambda i, f, be, nu, nv: (layer, be[cl(i, nu)], f, 0))
    bspec_in = pl.BlockSpec((1, 1, tf), lambda i, f, be, nu, nv: (layer * n_e + be[cl(i, nu)], 0, f))
    bspec_out = pl.BlockSpec((1, 1, d), lambda i, f, be, nu, nv: (layer * n_e + be[cl(i, nu)], 0, 0))
    grid_spec = pltpu.PrefetchScalarGridSpec(
        num_scalar_prefetch=3,
        grid=(nb, ff // tf),
        in_specs=[
            pl.BlockSpec((tm, half), lambda i, f, be, nu, nv: (cl(i, nu), 0)),
            wspec_in, bspec_in, wspec_in, bspec_in, wspec_out, bspec_out,
        ],
        out_specs=pl.BlockSpec((tm, d), lambda i, f, be, nu, nv: (i, 0)),
        scratch_shapes=[pltpu.VMEM((tm, d), BF16)],
    )
    return pl.pallas_call(
        _expert_kernel,
        grid_spec=grid_spec,
        out_shape=jax.ShapeDtypeStruct((cap, d), F32),
        compiler_params=_cparams(("parallel", "arbitrary")),
        name="experts",
    )(block_e, n_used, n_valid, xb, w_gate, b_gate.reshape(-1, 1, ff), w_up, b_up.reshape(-1, 1, ff),
      w_down, b_down.reshape(-1, 1, d))


def _combine_kernel(dcur_ref, dnxt_ref, w_ref, x_ref, mod_ref, yb_ref, o_ref, ybuf, sems, *, kg):
    i = pl.program_id(0)
    n = pl.num_programs(0)
    tm = x_ref.shape[0]
    slot = i % 2

    def fetch(d_ref, s):
        for t in range(tm):
            for k in range(TOP_K):
                pltpu.make_async_copy(yb_ref.at[pl.ds(d_ref[k, t], 1)], ybuf.at[s, k, pl.ds(t, 1)],
                                      sems.at[s]).start()

    @pl.when(i == 0)
    def _():
        fetch(dcur_ref, 0)

    for s in range(2):
        @pl.when(jnp.logical_and(i + 1 < n, slot == 1 - s))
        def _():
            fetch(dnxt_ref, s)

    for k in range(TOP_K):
        pltpu.make_async_copy(yb_ref.at[pl.ds(0, tm)], ybuf.at[slot, k], sems.at[slot]).wait()
    w = w_ref[...]
    f = ybuf[slot, 0] * w[:, 0:1]
    for k in range(1, TOP_K):
        f = f + ybuf[slot, k] * w[:, k:k + 1]
    o_ref[...] = x_ref[...] + mod_ref[0][kg:kg + 1, :] * f


def _combine(geo, yb, dest, w_tk, x, mod, kg):
    t, d = x.shape
    tm = _pick(128, geo.t_ctx, geo.seq_lat)
    nsteps = t // tm
    row = geo.mod_row(tm)
    return pl.pallas_call(
        functools.partial(_combine_kernel, kg=kg),
        grid=(nsteps,),
        in_specs=[
            pl.BlockSpec((TOP_K, tm), lambda i: (0, i), memory_space=pltpu.SMEM),
            pl.BlockSpec((TOP_K, tm), lambda i: (0, jnp.minimum(i + 1, nsteps - 1)), memory_space=pltpu.SMEM),
            pl.BlockSpec((tm, TOP_K), lambda i: (i, 0)),
            pl.BlockSpec((tm, d), lambda i: (i, 0)),
            pl.BlockSpec((1, 6, d), lambda i: (row(i), 0, 0)),
            pl.BlockSpec(memory_space=pl.ANY),
        ],
        out_specs=pl.BlockSpec((tm, d), lambda i: (i, 0)),
        out_shape=jax.ShapeDtypeStruct((t, d), F32),
        scratch_shapes=[pltpu.VMEM((2, TOP_K, tm, d), F32), pltpu.SemaphoreType.DMA((2,))],
        compiler_params=_dma_cparams(("arbitrary",)),
        name="combine",
    )(dest, dest, w_tk, x, mod, yb)


MOE_TM = 1024


def _moe_blocks(t, n_e):
    return -(-(t * TOP_K) // MOE_TM) + n_e


def _moe(geo, x, nw, mod, layer, w_router, b_router, w_gate, b_gate, w_up, b_up, w_down, b_down, xb_buf):
    t, d = x.shape
    n_e = w_router.shape[1]
    hp, top_e, top_w, rank, cnt = _router(geo, x, nw, mod, w_router.T, b_router, 3)
    counts = cnt[:, 0].astype(I32)
    padded = (counts + MOE_TM - 1) // MOE_TM * MOE_TM
    pad_end = jnp.cumsum(padded)
    pad_start = pad_end - padded
    nb = _moe_blocks(t, n_e)
    experts = jnp.arange(n_e, dtype=I32)
    start_of = jnp.sum(jnp.where(top_e[None] == experts[:, None, None], pad_start[:, None, None], 0), axis=0)
    dest = start_of + rank
    blk0 = jnp.arange(nb, dtype=I32) * MOE_TM
    block_e = jnp.minimum(jnp.sum((pad_end[None, :] <= blk0[:, None]).astype(I32), axis=1), n_e - 1)
    own = block_e[:, None] == experts[None, :]
    group_end = jnp.sum(jnp.where(own, (pad_start + counts)[None, :], 0), axis=1)
    n_valid = jnp.clip(group_end - blk0, 0, MOE_TM).astype(I32)
    n_used = (pad_end[-1:] // MOE_TM).astype(I32)
    xb = _dispatch(hp, dest, xb_buf)
    yb = _experts(xb, block_e, n_used, n_valid, layer, w_gate, b_gate, w_up, b_up, w_down, b_down, MOE_TM)
    return _combine(geo, yb, dest, top_w.T, x, mod, 5), xb


_SWAP = np.array([(j + ROPE_FREQ) if (j // ROPE_FREQ) % 2 == 0 else (j - ROPE_FREQ) for j in range(QK_ROPE)])


def _rope_table(seq_lat, tq):
    pos = jnp.arange(seq_lat)
    row = (pos // GRID_W).astype(F32)
    colp = (pos % GRID_W).astype(F32)
    inv = jnp.power(ROPE_THETA, -jnp.arange(ROPE_FREQ, dtype=F32) / ROPE_FREQ)
    ar, ac = row[:, None] * inv, colp[:, None] * inv
    cos = jnp.concatenate([jnp.cos(ar), jnp.cos(ar), jnp.cos(ac), jnp.cos(ac)], axis=1)
    sin = jnp.concatenate([-jnp.sin(ar), jnp.sin(ar), -jnp.sin(ac), jnp.sin(ac)], axis=1)
    ident = jnp.concatenate([jnp.ones((tq, QK_ROPE), F32), jnp.zeros((tq, QK_ROPE), F32)], axis=1)
    return jnp.concatenate([ident, jnp.concatenate([cos, sin], axis=1)], axis=0)


def _even_params(w_in, w_qb, q_norm, q_rope_norm, k_rope_norm):
    d = w_in.shape[0]
    o_kv, o_kr, o_conv = Q_LORA, Q_LORA + KV_LORA, Q_LORA + KV_LORA + QK_ROPE
    kr = w_in[:, o_kr:o_conv]
    pad = jnp.zeros((d, EV_N - EV_KR - 2 * QK_ROPE), w_in.dtype)
    w_in_r = jnp.concatenate([w_in[:, o_conv:], w_in[:, :o_kr], kr, kr[:, _SWAP], pad], axis=1).astype(BF16)
    wq = w_qb.reshape(Q_LORA, MLA_HEADS, QK_NOPE + QK_ROPE)
    wq_r = jnp.concatenate([wq, wq[:, :, QK_NOPE:][:, :, _SWAP]], axis=2).reshape(Q_LORA, MLA_HEADS * 256)
    q_nw = jnp.concatenate([q_norm, q_rope_norm, q_rope_norm[_SWAP]]).reshape(1, 256)
    kr_nw = jnp.concatenate([k_rope_norm, k_rope_norm[_SWAP]]).reshape(1, 2 * QK_ROPE)
    return w_in_r, wq_r.astype(BF16), q_nw, kr_nw


def _odd_params(w_in, w_gk, b_gk):
    d = w_in.shape[0]
    pad = jnp.zeros((d, OD_N - w_in.shape[1]), w_in.dtype)
    w_in_r = jnp.concatenate([w_in, pad], axis=1).astype(BF16)
    wpad = jnp.zeros((2, LANES, w_gk.shape[2]), F32)
    wpad = wpad.at[0, :GATE_RANK].set(w_gk[0]).at[1, GATE_RANK:2 * GATE_RANK].set(w_gk[1])
    return w_in_r, wpad.astype(BF16), b_gk.reshape(2, 1, -1)


def kernel(x_prompt, x_sample, cache_mla_ckv, cache_mla_krope, state_gla, c, c_ctx, ada_w, ada_b, norm_mix_w, norm_ffn_w, ev_w_in, ev_q_a_norm, ev_w_qb, ev_kv_a_norm, ev_w_kvb, ev_q_norm, ev_k_norm, ev_q_rope_norm, ev_k_rope_norm, ev_conv_w, ev_w_out, od_w_in, od_w_gk, od_b_gk, od_o_norm, od_w_out, moe_w_router, moe_b_router, moe_w_gate, moe_b_gate, moe_w_up, moe_b_up, moe_w_down, moe_b_down):
    geo = _Geo(x_prompt, x_sample, cache_mla_ckv)
    d = geo.d
    depth = ada_w.shape[0]
    x = jnp.concatenate([x_prompt.reshape(geo.t_ctx, d), x_sample.reshape(geo.t_lat, d)], axis=0)
    cond = jnp.concatenate([c_ctx[None, :], c, jnp.zeros((16 - 1 - geo.n_lat, d), F32)], axis=0)
    mods = _adaln(cond, ada_w, ada_b).reshape(depth, 16, 6, d)
    tq = _pick(512, geo.t_ctx, geo.seq_lat)
    tab = _rope_table(geo.seq_lat, tq)
    w_gate, w_up, w_down = moe_w_gate.astype(BF16), moe_w_up.astype(BF16), moe_w_down.astype(BF16)
    xb_buf = jnp.zeros((_moe_blocks(geo.t, moe_w_router.shape[2]) * MOE_TM, d // 2), U32)
    attn_buf = jnp.zeros((geo.t, MLA_HEADS * V_HEAD), BF16)
    gla_bufs = (jnp.zeros((geo.t, GLA_HEADS * GLA_DVH), BF16), jnp.zeros((geo.t, GLA_HEADS * GLA_DVH), BF16))
    new_ckv, new_krope, new_gla = [], [], []
    for l in range(depth):
        i = l // 2
        mod = mods[l]
        if l % 2 == 0:
            w_in_r, wq_r, q_nw, kr_nw = _even_params(ev_w_in[i], ev_w_qb[i], ev_q_norm[i], ev_q_rope_norm[i],
                                                     ev_k_rope_norm[i])
            z = _lnmod_mm(geo, x, norm_mix_w[l], mod, w_in_r, 0)
            conv, qan, ckv, krn, krot = _even_mid(geo, z, ev_conv_w[i], ev_q_a_norm[i], ev_kv_a_norm[i], kr_nw,
                                                  tab, tq)
            q256 = _q_proj(geo, qan, wq_r, q_nw, tab, tq)
            kc = cache_mla_krope[:, i].reshape(geo.n_lat * geo.past, QK_ROPE)
            ckv_all = jnp.concatenate([ckv, cache_mla_ckv[:, i].reshape(geo.n_lat * geo.past, KV_LORA)], axis=0)
            krot_all = jnp.concatenate([krot, jnp.concatenate([kc, kc], axis=1).astype(BF16)], axis=0)
            k256, v256 = _kv_proj(ckv_all, ev_w_kvb[i].astype(BF16), ev_k_norm[i], krot_all)
            attn_buf = _attention(geo, q256, k256, v256, attn_buf)
            x = _mix_out(geo, attn_buf, conv, ev_w_out[i].astype(BF16), x, mod, 2)
            new_ckv.append(ckv[:geo.t_ctx].reshape(geo.n_ctx, geo.seq_ctx, KV_LORA))
            new_krope.append(krn[:geo.t_ctx].reshape(geo.n_ctx, geo.seq_ctx, QK_ROPE))
        else:
            w_in_r, wgk_pad, bgk = _odd_params(od_w_in[i], od_w_gk[i], od_b_gk[i])
            z = _lnmod_mm(geo, x, norm_mix_w[l], mod, w_in_r, 0)
            o_f, o_b, s_fin = _gla_scan(z, wgk_pad, bgk, geo.n_ctx, geo.seq_ctx, 0, None, True, gla_bufs)
            gla_bufs = _gla_scan(z, wgk_pad, bgk, geo.n_lat, geo.seq_lat, geo.t_ctx, state_gla[:, i], False,
                                 (o_f, o_b))
            x = _gla_out(geo, gla_bufs[0], gla_bufs[1], z, od_o_norm[i], od_w_out[i].astype(BF16), x, mod, 2)
            new_gla.append(s_fin)
        x, xb_buf = _moe(geo, x, norm_ffn_w[l], mod, l, moe_w_router[l], moe_b_router[l], w_gate, moe_b_gate,
                         w_up, moe_b_up, w_down, moe_b_down, xb_buf)
    xp = x[:geo.t_ctx].reshape(x_prompt.shape)
    xs = x[geo.t_ctx:].reshape(x_sample.shape)
    return (xp, xs, jnp.stack(new_ckv, axis=1), jnp.stack(new_krope, axis=1), jnp.stack(new_gla, axis=1))
```

```python
import functools

import jax
import jax.numpy as jnp
import numpy as np
from jax import lax
from jax.experimental import pallas as pl
from jax.experimental.pallas import tpu as pltpu

F32 = jnp.float32
BF16 = jnp.bfloat16
I32 = jnp.int32
U32 = jnp.uint32

EPS = 1e-6
GRID_W = 64
MLA_HEADS = 8
QK_NOPE = 128
QK_ROPE = 64
V_HEAD = 128
Q_LORA = 512
KV_LORA = 256
ROPE_THETA = 10000.0
ROPE_FREQ = QK_ROPE // 4
CONV_W = 1024
GLA_HEADS = 4
GLA_DKH = 256
GLA_DVH = 512
GATE_RANK = 16
GATE_NORM = 16.0
GLA_CHUNK = 64
TOP_K = 4
SWIGLU_ALPHA = 1.702
SWIGLU_LIMIT = 7.0

V7X_VMEM_LIMIT = 56 * 1024 * 1024
LANES = 128
HI = lax.Precision.HIGHEST


def _cparams(sem):
    return pltpu.CompilerParams(dimension_semantics=sem, vmem_limit_bytes=V7X_VMEM_LIMIT)


def _pick(want, *dims):
    t = want
    while any(d % t for d in dims):
        t //= 2
    return t


def _dot(a, b):
    return jnp.dot(a, b, preferred_element_type=F32)


def _rms(x, w):
    return x * lax.rsqrt(jnp.mean(x * x, axis=-1, keepdims=True) + EPS) * w


def _lnmod(x, nw, mod, ks):
    gain = nw * (1.0 + mod[ks + 1:ks + 2, :])
    return x * lax.rsqrt(jnp.mean(x * x, axis=-1, keepdims=True) + EPS) * gain + mod[ks:ks + 1, :]


class _Geo:
    def __init__(self, x_prompt, x_sample, cache_ckv):
        self.n_ctx, self.seq_ctx, self.d = x_prompt.shape
        self.n_lat, self.seq_lat, _ = x_sample.shape
        self.past = cache_ckv.shape[2]
        self.t_ctx = self.n_ctx * self.seq_ctx
        self.t_lat = self.n_lat * self.seq_lat
        self.t = self.t_ctx + self.t_lat

    def mod_row(self, tm):
        t_ctx, seq_lat = self.t_ctx, self.seq_lat
        return lambda i: jnp.where(i * tm < t_ctx, 0, 1 + (i * tm - t_ctx) // seq_lat)


def _adaln_kernel(c_ref, w_ref, b_ref, o_ref):
    c = c_ref[...]
    a = (c * jax.nn.sigmoid(c)).astype(BF16)
    o_ref[0] = _dot(a, w_ref[0].astype(BF16)) + b_ref[0]


def _adaln(cond16, ada_w, ada_b):
    depth, d, n = ada_w.shape
    tn = 1024
    return pl.pallas_call(
        _adaln_kernel,
        grid=(depth, n // tn),
        in_specs=[
            pl.BlockSpec((16, d), lambda l, j: (0, 0)),
            pl.BlockSpec((1, d, tn), lambda l, j: (l, 0, j)),
            pl.BlockSpec((1, 1, tn), lambda l, j: (l, 0, j)),
        ],
        out_specs=pl.BlockSpec((1, 16, tn), lambda l, j: (l, 0, j)),
        out_shape=jax.ShapeDtypeStruct((depth, 16, n), F32),
        compiler_params=_cparams(("parallel", "parallel")),
        name="adaln",
    )(cond16, ada_w, ada_b.reshape(depth, 1, n))


def _lnmod_mm_kernel(x_ref, nw_ref, mod_ref, w_ref, o_ref, h_scr, *, ks):
    @pl.when(pl.program_id(1) == 0)
    def _():
        h_scr[...] = _lnmod(x_ref[...], nw_ref[...], mod_ref[0], ks).astype(BF16)

    o_ref[...] = _dot(h_scr[...], w_ref[...])


def _lnmod_mm(geo, x, nw, mod, w, ks):
    t, d = x.shape
    n = w.shape[1]
    tm = _pick(1024, geo.t_ctx, geo.seq_lat)
    tn = max(c for c in range(LANES, 1024 + 1, LANES) if n % c == 0)
    row = geo.mod_row(tm)
    return pl.pallas_call(
        functools.partial(_lnmod_mm_kernel, ks=ks),
        grid=(t // tm, n // tn),
        in_specs=[
            pl.BlockSpec((tm, d), lambda i, j: (i, 0)),
            pl.BlockSpec((1, d), lambda i, j: (0, 0)),
            pl.BlockSpec((1, 6, d), lambda i, j: (row(i), 0, 0)),
            pl.BlockSpec((d, tn), lambda i, j: (0, j)),
        ],
        out_specs=pl.BlockSpec((tm, tn), lambda i, j: (i, j)),
        out_shape=jax.ShapeDtypeStruct((t, n), F32),
        scratch_shapes=[pltpu.VMEM((tm, d), BF16)],
        compiler_params=_cparams(("parallel", "arbitrary")),
        name="lnmod_mm",
    )(x, nw.reshape(1, d), mod, w)


EV_QA = 3 * CONV_W
EV_KVA = EV_QA + Q_LORA
EV_KR = EV_KVA + KV_LORA
EV_N = 4096


def _even_mid_kernel(bg_ref, cg_ref, xv_ref, cgp_ref, xvp_ref, cgn_ref, xvn_ref, qa_ref, kva_ref, kr_ref,
                     cw_ref, qaw_ref, kvw_ref, krw_ref, tab_ref,
                     conv_ref, qan_ref, ckv_ref, krn_ref, krot_ref, *, t_ctx, seq_ctx, seq_lat):
    i = pl.program_id(0)
    tm = bg_ref.shape[0]
    u = cg_ref[...] * xv_ref[...]
    rows = lax.broadcasted_iota(I32, (tm, 1), 0)
    tok = i * tm + rows
    pos = jnp.where(tok < t_ctx, tok & (seq_ctx - 1), (tok - t_ctx) & (seq_lat - 1))
    last = jnp.where(tok < t_ctx, seq_ctx - 1, seq_lat - 1)
    u_m1 = jnp.where(rows == 0, cgp_ref[7:8, :] * xvp_ref[7:8, :], pltpu.roll(u, 1, axis=0))
    u_p1 = jnp.where(rows == tm - 1, cgn_ref[0:1, :] * xvn_ref[0:1, :], pltpu.roll(u, tm - 1, axis=0))
    u_m1 = jnp.where(pos == 0, 0.0, u_m1)
    u_p1 = jnp.where(pos == last, 0.0, u_p1)
    cw = cw_ref[...]
    conv = bg_ref[...] * (u_m1 * cw[0:1, :] + u * cw[1:2, :] + u_p1 * cw[2:3, :])
    conv_ref[...] = conv.astype(BF16)
    qan_ref[...] = _rms(qa_ref[...], qaw_ref[...]).astype(BF16)
    ckv_ref[...] = _rms(kva_ref[...], kvw_ref[...])
    kr = kr_ref[...]
    krn = _rms(kr, krw_ref[...])
    krn_ref[...] = krn[:, :QK_ROPE]
    y = krn * tab_ref[...]
    krot_ref[...] = (y + pltpu.roll(y, QK_ROPE, axis=1)).astype(BF16)


def _even_mid(geo, z, conv_w, qa_w, kv_w, kr_w128, tab, tq):
    t = z.shape[0]
    tm = tq
    ncb = geo.t_ctx // tm
    bps = geo.seq_lat // tm
    nb8 = t // 8
    r8 = tm // 8
    assert geo.seq_ctx & (geo.seq_ctx - 1) == 0 and geo.seq_lat & (geo.seq_lat - 1) == 0
    tab_idx = lambda i: jnp.where(i < ncb, 0, 1 + (i - ncb) % bps)
    col = lambda c: (lambda i: (i, c))
    in_specs = [
        pl.BlockSpec((tm, CONV_W), col(0)),
        pl.BlockSpec((tm, CONV_W), col(1)),
        pl.BlockSpec((tm, CONV_W), col(2)),
        pl.BlockSpec((8, CONV_W), lambda i: (jnp.maximum(i * r8 - 1, 0), 1)),
        pl.BlockSpec((8, CONV_W), lambda i: (jnp.maximum(i * r8 - 1, 0), 2)),
        pl.BlockSpec((8, CONV_W), lambda i: (jnp.minimum((i + 1) * r8, nb8 - 1), 1)),
        pl.BlockSpec((8, CONV_W), lambda i: (jnp.minimum((i + 1) * r8, nb8 - 1), 2)),
        pl.BlockSpec((tm, Q_LORA), col(EV_QA // Q_LORA)),
        pl.BlockSpec((tm, KV_LORA), col(EV_KVA // KV_LORA)),
        pl.BlockSpec((tm, 2 * QK_ROPE), col(EV_KR // (2 * QK_ROPE))),
        pl.BlockSpec((3, CONV_W), lambda i: (0, 0)),
        pl.BlockSpec((1, Q_LORA), lambda i: (0, 0)),
        pl.BlockSpec((1, KV_LORA), lambda i: (0, 0)),
        pl.BlockSpec((1, 2 * QK_ROPE), lambda i: (0, 0)),
        pl.BlockSpec((tm, 2 * QK_ROPE), lambda i: (tab_idx(i), 0)),
    ]
    out_specs = [
        pl.BlockSpec((tm, CONV_W), lambda i: (i, 0)),
        pl.BlockSpec((tm, Q_LORA), lambda i: (i, 0)),
        pl.BlockSpec((tm, KV_LORA), lambda i: (i, 0)),
        pl.BlockSpec((tm, QK_ROPE), lambda i: (i, 0)),
        pl.BlockSpec((tm, 2 * QK_ROPE), lambda i: (i, 0)),
    ]
    out_shape = [
        jax.ShapeDtypeStruct((t, CONV_W), BF16),
        jax.ShapeDtypeStruct((t, Q_LORA), BF16),
        jax.ShapeDtypeStruct((t, KV_LORA), F32),
        jax.ShapeDtypeStruct((t, QK_ROPE), F32),
        jax.ShapeDtypeStruct((t, 2 * QK_ROPE), BF16),
    ]
    return pl.pallas_call(
        functools.partial(_even_mid_kernel, t_ctx=geo.t_ctx, seq_ctx=geo.seq_ctx, seq_lat=geo.seq_lat),
        grid=(t // tm,),
        in_specs=in_specs,
        out_specs=out_specs,
        out_shape=out_shape,
        compiler_params=_cparams(("parallel",)),
        name="even_mid",
    )(z, z, z, z, z, z, z, z, z, z, conv_w, qa_w.reshape(1, -1), kv_w.reshape(1, -1), kr_w128, tab)


def _q_proj_kernel(a_ref, w_ref, nw_ref, tab_ref, o_ref):
    acc = _dot(a_ref[...], w_ref[...])
    nw = nw_ref[...]
    tab = tab_ref[...]
    for h in range(MLA_HEADS):
        n = acc[:, h * 256:h * 256 + QK_NOPE]
        r = acc[:, h * 256 + QK_NOPE:(h + 1) * 256]
        qn = _rms(n, nw[:, :QK_NOPE])
        y = _rms(r, nw[:, QK_NOPE:]) * tab
        o_ref[h] = (jnp.concatenate([qn, y], axis=-1) * Q_PRESCALE).astype(BF16)


def _q_proj(geo, qan, w_qb, nw256, tab, tq):
    t = qan.shape[0]
    tm = tq
    ncb = geo.t_ctx // tm
    bps = geo.seq_lat // tm
    n = w_qb.shape[1]
    return pl.pallas_call(
        _q_proj_kernel,
        grid=(t // tm,),
        in_specs=[
            pl.BlockSpec((tm, Q_LORA), lambda i: (i, 0)),
            pl.BlockSpec((Q_LORA, n), lambda i: (0, 0)),
            pl.BlockSpec((1, 256), lambda i: (0, 0)),
            pl.BlockSpec((tm, 2 * QK_ROPE), lambda i: (jnp.where(i < ncb, 0, 1 + (i - ncb) % bps), 0)),
        ],
        out_specs=pl.BlockSpec((MLA_HEADS, tm, 256), lambda i: (0, i, 0)),
        out_shape=jax.ShapeDtypeStruct((MLA_HEADS, t, 256), BF16),
        compiler_params=_cparams(("parallel",)),
        name="q_proj",
    )(qan, w_qb, nw256, tab)


def _kv_proj_kernel(a_ref, w_ref, nw_ref, kr_ref, k_ref, v_ref):
    acc = _dot(a_ref[...].astype(BF16), w_ref[...])
    kr = kr_ref[...]
    ones = jnp.ones((acc.shape[0], V_HEAD), BF16)
    for h in range(MLA_HEADS):
        kn = _rms(acc[:, h * 256:h * 256 + QK_NOPE], nw_ref[...]).astype(BF16)
        k_ref[h] = jnp.concatenate([kn, kr], axis=-1)
        v_ref[h] = jnp.concatenate([acc[:, h * 256 + QK_NOPE:(h + 1) * 256].astype(BF16), ones], axis=-1)


def _kv_proj(ckv_all, w_kvb, k_norm, krot_all):
    t = ckv_all.shape[0]
    tm = _pick(512, t)
    n = w_kvb.shape[1]
    return pl.pallas_call(
        _kv_proj_kernel,
        grid=(t // tm,),
        in_specs=[
            pl.BlockSpec((tm, KV_LORA), lambda i: (i, 0)),
            pl.BlockSpec((KV_LORA, n), lambda i: (0, 0)),
            pl.BlockSpec((1, QK_NOPE), lambda i: (0, 0)),
            pl.BlockSpec((tm, 2 * QK_ROPE), lambda i: (i, 0)),
        ],
        out_specs=[
            pl.BlockSpec((MLA_HEADS, tm, 256), lambda i: (0, i, 0)),
            pl.BlockSpec((MLA_HEADS, tm, 2 * V_HEAD), lambda i: (0, i, 0)),
        ],
        out_shape=[
            jax.ShapeDtypeStruct((MLA_HEADS, t, 256), BF16),
            jax.ShapeDtypeStruct((MLA_HEADS, t, 2 * V_HEAD), BF16),
        ],
        compiler_params=_cparams(("parallel",)),
        name="kv_proj",
    )(ckv_all, w_kvb, k_norm.reshape(1, -1), krot_all)


_NT = (((1,), (1,)), ((), ()))
_TN = (((0,), (0,)), ((), ()))
Q_PRESCALE = (QK_NOPE + QK_ROPE) ** -0.5 * 1.4426950408889634
ATTN_KEY_CHUNK = 512


def _softmax_step(q, k, v, m, acc):
    s = lax.dot_general(q, k, _NT, preferred_element_type=F32)
    m_new = jnp.maximum(m, jnp.max(s, axis=-1, keepdims=True))
    p = jnp.exp2(s - m_new)
    acc = acc * jnp.exp2(m - m_new) + _dot(p.astype(BF16), v)
    return m_new, acc


def _attn_ctx_kernel(q_ref, k_ref, v_ref, prev_ref, o_ref):
    del prev_ref
    outs = []
    for h in range(MLA_HEADS):
        s = lax.dot_general(q_ref[h], k_ref[h], _NT, preferred_element_type=F32)
        p = jnp.exp2(s - jnp.max(s, axis=-1, keepdims=True))
        o = _dot(p.astype(BF16), v_ref[h])
        outs.append(o[:, :V_HEAD] / o[:, V_HEAD:V_HEAD + 1])
    o_ref[...] = jnp.concatenate(outs, axis=-1).astype(BF16)


def _attn_lat_kernel(q_ref, kc_ref, vc_ref, kl_ref, vl_ref, prev_ref, o_ref):
    del prev_ref
    q = q_ref[0]
    tq = q.shape[0]
    m = jnp.full((tq, 1), -jnp.inf, F32)
    acc = jnp.zeros((tq, 2 * V_HEAD), F32)
    for k_ref, v_ref in ((kc_ref, vc_ref), (kl_ref, vl_ref)):
        n = k_ref.shape[1]
        ck = min(ATTN_KEY_CHUNK, n)
        for j in range(n // ck):
            m, acc = _softmax_step(q, k_ref[0, j * ck:(j + 1) * ck, :], v_ref[0, j * ck:(j + 1) * ck, :], m, acc)
    o_ref[...] = (acc[:, :V_HEAD] / acc[:, V_HEAD:V_HEAD + 1]).astype(BF16)


def _attention(geo, q256, k256, v256, o_buf):
    t = geo.t
    sc, sl, past = geo.seq_ctx, geo.seq_lat, geo.past
    hv = MLA_HEADS * V_HEAD
    vw = 2 * V_HEAD
    o_ctx = pl.pallas_call(
        _attn_ctx_kernel,
        grid=(geo.n_ctx,),
        in_specs=[
            pl.BlockSpec((MLA_HEADS, sc, 256), lambda b: (0, b, 0)),
            pl.BlockSpec((MLA_HEADS, sc, 256), lambda b: (0, b, 0)),
            pl.BlockSpec((MLA_HEADS, sc, vw), lambda b: (0, b, 0)),
            pl.BlockSpec(memory_space=pl.ANY),
        ],
        out_specs=pl.BlockSpec((sc, hv), lambda b: (b, 0)),
        out_shape=jax.ShapeDtypeStruct((t, hv), BF16),
        input_output_aliases={3: 0},
        compiler_params=_cparams(("parallel",)),
        name="attn_ctx",
    )(q256, k256, v256, o_buf)
    tq = _pick(512, sl)
    nq = sl // tq
    q_off = geo.t_ctx // tq
    kl_off = geo.t_ctx // sl
    kc_off = t // past
    return pl.pallas_call(
        _attn_lat_kernel,
        grid=(geo.n_lat, MLA_HEADS, nq),
        in_specs=[
            pl.BlockSpec((1, tq, 256), lambda b, h, i: (h, q_off + b * nq + i, 0)),
            pl.BlockSpec((1, past, 256), lambda b, h, i: (h, kc_off + b, 0)),
            pl.BlockSpec((1, past, vw), lambda b, h, i: (h, kc_off + b, 0)),
            pl.BlockSpec((1, sl, 256), lambda b, h, i: (h, kl_off + b, 0)),
            pl.BlockSpec((1, sl, vw), lambda b, h, i: (h, kl_off + b, 0)),
            pl.BlockSpec(memory_space=pl.ANY),
        ],
        out_specs=pl.BlockSpec((tq, V_HEAD), lambda b, h, i: (q_off + b * nq + i, h)),
        out_shape=jax.ShapeDtypeStruct((t, hv), BF16),
        input_output_aliases={5: 0},
        compiler_params=_cparams(("parallel", "parallel", "arbitrary")),
        name="attn_lat",
    )(q256, k256, v256, k256, v256, o_ctx)


def _mix_out_kernel(a1_ref, a2_ref, w_ref, x_ref, mod_ref, o_ref, *, kg):
    k1 = a1_ref.shape[1]
    acc = _dot(a1_ref[...], w_ref[:k1, :]) + _dot(a2_ref[...], w_ref[k1:, :])
    o_ref[...] = x_ref[...] + mod_ref[0][kg:kg + 1, :] * acc


def _mix_out(geo, a1, a2, w, x, mod, kg):
    t, d = x.shape
    tm = _pick(512, geo.t_ctx, geo.seq_lat)
    row = geo.mod_row(tm)
    k1, k2 = a1.shape[1], a2.shape[1]
    return pl.pallas_call(
        functools.partial(_mix_out_kernel, kg=kg),
        grid=(t // tm,),
        in_specs=[
            pl.BlockSpec((tm, k1), lambda i: (i, 0)),
            pl.BlockSpec((tm, k2), lambda i: (i, 0)),
            pl.BlockSpec((k1 + k2, d), lambda i: (0, 0)),
            pl.BlockSpec((tm, d), lambda i: (i, 0)),
            pl.BlockSpec((1, 6, d), lambda i: (row(i), 0, 0)),
        ],
        out_specs=pl.BlockSpec((tm, d), lambda i: (i, 0)),
        out_shape=jax.ShapeDtypeStruct((t, d), F32),
        compiler_params=_cparams(("parallel",)),
        name="mix_out",
    )(a1, a2, w, x, mod)


OD_K = GLA_HEADS * GLA_DKH
OD_V = 2 * OD_K
OD_G = OD_V + GLA_HEADS * GLA_DVH
OD_R = OD_G + GLA_HEADS * GLA_DVH
OD_N = OD_R + LANES


def _log_sigmoid(x):
    return jnp.minimum(x, 0.0) - jnp.log1p(jnp.exp(-jnp.abs(x)))


def _split3(x):
    hi = x.astype(BF16)
    r1 = x - hi.astype(F32)
    mid = r1.astype(BF16)
    lo = (r1 - mid.astype(F32)).astype(BF16)
    return hi, mid, lo


def _gla_chunk(q, k, v, r, wgk, bgk, st_ref, o_ref, backward):
    c = q.shape[0]
    la = _log_sigmoid(_dot(r.astype(BF16), wgk) + bgk) * (1.0 / GATE_NORM)
    ii = lax.broadcasted_iota(I32, (c, c), 0)
    jj = lax.broadcasted_iota(I32, (c, c), 1)
    causal = (ii <= jj) if backward else (ii >= jj)
    tri = causal.astype(BF16)
    hi, mid, lo = _split3(la)
    b = _dot(tri, hi) + _dot(tri, mid) + _dot(tri, lo)
    tot = jnp.sum(la, axis=0, keepdims=True)
    q_dec = (q * (GLA_DKH ** -0.5) * jnp.exp(b)).astype(BF16)
    k_dec = (k * jnp.exp(-b)).astype(BF16)
    k_end = (k * jnp.exp(tot - b)).astype(BF16)
    e_tot = jnp.exp(tot)
    vb = v.astype(BF16)
    heads = range(GLA_HEADS)
    ks = [slice(h * GLA_DKH, (h + 1) * GLA_DKH) for h in heads]
    vs = [slice(h * GLA_DVH, (h + 1) * GLA_DVH) for h in heads]
    scores = [jnp.where(causal, lax.dot_general(q_dec[:, ks[h]], k_dec[:, ks[h]], _NT, preferred_element_type=F32),
                        0.0).astype(BF16) for h in heads]
    st = [st_ref[h] for h in heads]
    for h in heads:
        o_ref[:, vs[h]] = (_dot(scores[h], vb[:, vs[h]]) + lax.dot_general(
            q_dec[:, ks[h]], st[h].astype(BF16), _NT, preferred_element_type=F32)).astype(o_ref.dtype)
    for h in heads:
        st_ref[h] = e_tot[:, ks[h]] * st[h] + lax.dot_general(vb[:, vs[h]], k_end[:, ks[h]], _TN,
                                                             preferred_element_type=F32)


def _gla_kernel(*refs, has_init, want_final):
    qf, kf, vf, rf, qb, kb, vb, rb, wgk_ref, bgk_ref = refs[:10]
    pos = 10
    s0_ref = None
    if has_init:
        s0_ref = refs[pos]
        pos += 1
    pos += 2
    of_ref, ob_ref = refs[pos], refs[pos + 1]
    pos += 2
    sfin_ref = None
    if want_final:
        sfin_ref = refs[pos]
        pos += 1
    st_scr = refs[pos]
    c = pl.program_id(1)
    nc = pl.num_programs(1)

    @pl.when(c == 0)
    def _():
        for d in range(2):
            for h in range(GLA_HEADS):
                if has_init:
                    st_scr[d, h] = s0_ref[0, d, h].T
                else:
                    st_scr[d, h] = jnp.zeros(st_scr.shape[2:], F32)

    _gla_chunk(qf[...], kf[...], vf[...], rf[...], wgk_ref[0], bgk_ref[0], st_scr.at[0], of_ref, False)
    _gla_chunk(qb[...], kb[...], vb[...], rb[...], wgk_ref[1], bgk_ref[1], st_scr.at[1], ob_ref, True)

    if want_final:
        @pl.when(c == nc - 1)
        def _():
            for d in range(2):
                for h in range(GLA_HEADS):
                    sfin_ref[0, d, h] = st_scr[d, h].T


def _gla_scan(z, wgk_pad, bgk, n_seq, seq, row0, s0, want_final, bufs):
    t = z.shape[0]
    c = GLA_CHUNK
    nc = seq // c
    base = row0 // c
    fwd = lambda b, i: base + b * nc + i
    bwd = lambda b, i: base + b * nc + (nc - 1 - i)
    dk, dv = GLA_HEADS * GLA_DKH, GLA_HEADS * GLA_DVH

    def zspecs(rowf):
        return [
            pl.BlockSpec((c, dk), lambda b, i: (rowf(b, i), 0)),
            pl.BlockSpec((c, dk), lambda b, i: (rowf(b, i), OD_K // dk)),
            pl.BlockSpec((c, dv), lambda b, i: (rowf(b, i), OD_V // dv)),
            pl.BlockSpec((c, LANES), lambda b, i: (rowf(b, i), OD_R // LANES)),
        ]

    in_specs = zspecs(fwd) + zspecs(bwd) + [
        pl.BlockSpec((2, LANES, dk), lambda b, i: (0, 0, 0)),
        pl.BlockSpec((2, 1, dk), lambda b, i: (0, 0, 0)),
    ]
    args = [z] * 8 + [wgk_pad, bgk]
    if s0 is not None:
        in_specs.append(pl.BlockSpec((1, 2, GLA_HEADS, GLA_DKH, GLA_DVH), lambda b, i: (b, 0, 0, 0, 0)))
        args.append(s0)
    aliases = {len(args): 0, len(args) + 1: 1}
    in_specs += [pl.BlockSpec(memory_space=pl.ANY)] * 2
    args += list(bufs)
    out_specs = [pl.BlockSpec((c, dv), lambda b, i: (fwd(b, i), 0)), pl.BlockSpec((c, dv), lambda b, i: (bwd(b, i), 0))]
    out_shape = [jax.ShapeDtypeStruct((t, dv), bufs[0].dtype)] * 2
    if want_final:
        out_specs.append(pl.BlockSpec((1, 2, GLA_HEADS, GLA_DKH, GLA_DVH), lambda b, i: (b, 0, 0, 0, 0)))
        out_shape.append(jax.ShapeDtypeStruct((n_seq, 2, GLA_HEADS, GLA_DKH, GLA_DVH), F32))
    return pl.pallas_call(
        functools.partial(_gla_kernel, has_init=s0 is not None, want_final=want_final),
        grid=(n_seq, nc),
        in_specs=in_specs,
        out_specs=out_specs,
        out_shape=out_shape,
        input_output_aliases=aliases,
        scratch_shapes=[pltpu.VMEM((2, GLA_HEADS, GLA_DVH, GLA_DKH), F32)],
        compiler_params=_cparams(("parallel", "arbitrary")),
        name="gla_scan",
    )(*args)


def _gla_out_kernel(of_ref, ob_ref, g_ref, nw_ref, w_ref, x_ref, mod_ref, o_ref, a_scr, *, kg):
    @pl.when(pl.program_id(1) == 0)
    def _():
        nw = nw_ref[...]
        for h in range(GLA_HEADS):
            sl = slice(h * GLA_DVH, (h + 1) * GLA_DVH)
            o = _rms(of_ref[:, sl].astype(F32) + ob_ref[:, sl].astype(F32), nw)
            g = g_ref[:, sl]
            a_scr[:, sl] = (o * (g * jax.nn.sigmoid(g))).astype(BF16)

    o_ref[...] = x_ref[...] + mod_ref[0][kg:kg + 1, :] * _dot(a_scr[...], w_ref[...])


def _gla_out(geo, o_f, o_b, z, o_norm, w_out, x, mod, kg):
    t, d = x.shape
    dv = GLA_HEADS * GLA_DVH
    tm = _pick(512, geo.t_ctx, geo.seq_lat)
    tn = d
    row = geo.mod_row(tm)
    return pl.pallas_call(
        functools.partial(_gla_out_kernel, kg=kg),
        grid=(t // tm, d // tn),
        in_specs=[
            pl.BlockSpec((tm, dv), lambda i, j: (i, 0)),
            pl.BlockSpec((tm, dv), lambda i, j: (i, 0)),
            pl.BlockSpec((tm, dv), lambda i, j: (i, OD_G // dv)),
            pl.BlockSpec((1, GLA_DVH), lambda i, j: (0, 0)),
            pl.BlockSpec((dv, tn), lambda i, j: (0, j)),
            pl.BlockSpec((tm, tn), lambda i, j: (i, j)),
            pl.BlockSpec((1, 6, tn), lambda i, j: (row(i), 0, j)),
        ],
        out_specs=pl.BlockSpec((tm, tn), lambda i, j: (i, j)),
        out_shape=jax.ShapeDtypeStruct((t, d), F32),
        scratch_shapes=[pltpu.VMEM((tm, dv), BF16)],
        compiler_params=_cparams(("parallel", "arbitrary")),
        name="gla_out",
    )(o_f, o_b, z, o_norm.reshape(1, -1), w_out, x, mod)


def _pack_bf16_pairs(h):
    n = h.shape[1] // 2
    lo = pltpu.bitcast(h[:, :n].astype(BF16).astype(F32), U32)
    hi = pltpu.bitcast(h[:, n:].astype(BF16).astype(F32), U32)
    return (lo >> 16) | (hi & jnp.uint32(0xFFFF0000))


def _unpack_bf16_pairs(w):
    lo = pltpu.bitcast(w << 16, F32).astype(BF16)
    hi = pltpu.bitcast(w & jnp.uint32(0xFFFF0000), F32).astype(BF16)
    return lo, hi


def _router_kernel(x_ref, nw_ref, mod_ref, wr_ref, br_ref, tri_ref,
                   hp_ref, e_ref, w_ref, rank_ref, cnt_ref, carry_scr, *, ks):
    @pl.when(pl.program_id(0) == 0)
    def _():
        carry_scr[...] = jnp.zeros(carry_scr.shape, F32)

    h = _lnmod(x_ref[...], nw_ref[...], mod_ref[0], ks)
    hp_ref[...] = _pack_bf16_pairs(h)
    lg = lax.dot_general(wr_ref[...], h, _NT, precision=HI, preferred_element_type=F32) + br_ref[...]
    n_e, tm = lg.shape
    rows = lax.broadcasted_iota(I32, (n_e, tm), 0).astype(F32)
    vals, sels, hits = [], [], []
    for _ in range(TOP_K):
        m = jnp.max(lg, axis=0, keepdims=True)
        idx = jnp.min(jnp.where(lg == m, rows, float(n_e)), axis=0, keepdims=True)
        hit = rows == idx
        vals.append(m)
        sels.append(idx)
        hits.append(hit)
        lg = jnp.where(hit, -jnp.inf, lg)
    ex = [jnp.exp(v - vals[0]) for v in vals]
    den = ex[0] + ex[1] + ex[2] + ex[3]
    w_ref[...] = jnp.concatenate([e / den for e in ex], axis=0)
    e_ref[...] = jnp.concatenate(sels, axis=0).astype(I32)
    onehot = jnp.zeros((n_e, tm), F32)
    for hit in hits:
        onehot = jnp.where(hit, 1.0, onehot)
    before = carry_scr[:, 0:1] + _dot(onehot.astype(BF16), tri_ref[...])
    rank_ref[...] = jnp.concatenate(
        [jnp.sum(jnp.where(hit, before, 0.0), axis=0, keepdims=True) for hit in hits], axis=0).astype(I32)
    carry_scr[...] = carry_scr[...] + jnp.sum(onehot, axis=1, keepdims=True)
    cnt_ref[...] = carry_scr[...]


def _router(geo, x, nw, mod, w_router_t, b_router, ks):
    t, d = x.shape
    n_e = w_router_t.shape[0]
    tm = _pick(512, geo.t_ctx, geo.seq_lat)
    row = geo.mod_row(tm)
    tri = (lax.broadcasted_iota(I32, (tm, tm), 0) < lax.broadcasted_iota(I32, (tm, tm), 1)).astype(BF16)
    return pl.pallas_call(
        functools.partial(_router_kernel, ks=ks),
        grid=(t // tm,),
        in_specs=[
            pl.BlockSpec((tm, d), lambda i: (i, 0)),
            pl.BlockSpec((1, d), lambda i: (0, 0)),
            pl.BlockSpec((1, 6, d), lambda i: (row(i), 0, 0)),
            pl.BlockSpec((n_e, d), lambda i: (0, 0)),
            pl.BlockSpec((n_e, 1), lambda i: (0, 0)),
            pl.BlockSpec((tm, tm), lambda i: (0, 0)),
        ],
        out_specs=[
            pl.BlockSpec((tm, d // 2), lambda i: (i, 0)),
            pl.BlockSpec((TOP_K, tm), lambda i: (0, i)),
            pl.BlockSpec((TOP_K, tm), lambda i: (0, i)),
            pl.BlockSpec((TOP_K, tm), lambda i: (0, i)),
            pl.BlockSpec((n_e, LANES), lambda i: (0, 0)),
        ],
        out_shape=[
            jax.ShapeDtypeStruct((t, d // 2), U32),
            jax.ShapeDtypeStruct((TOP_K, t), I32),
            jax.ShapeDtypeStruct((TOP_K, t), F32),
            jax.ShapeDtypeStruct((TOP_K, t), I32),
            jax.ShapeDtypeStruct((n_e, LANES), F32),
        ],
        scratch_shapes=[pltpu.VMEM((n_e, LANES), F32)],
        compiler_params=_cparams(("arbitrary",)),
        name="router",
    )(x, nw.reshape(1, d), mod, w_router_t, b_router.reshape(n_e, 1), tri)


def _dma_cparams(sem):
    return pltpu.CompilerParams(dimension_semantics=sem, vmem_limit_bytes=V7X_VMEM_LIMIT,
                                disable_bounds_checks=True)


def _dispatch_kernel(dest_ref, hp_ref, buf_ref, xb_ref, sem):
    del buf_ref
    tm = hp_ref.shape[0]

    for t in range(tm):
        for k in range(TOP_K):
            pltpu.make_async_copy(hp_ref.at[pl.ds(t, 1)], xb_ref.at[pl.ds(dest_ref[k, t], 1)], sem).start()
    for k in range(TOP_K):
        pltpu.make_async_copy(hp_ref, xb_ref.at[pl.ds(0, tm)], sem).wait()


def _dispatch(hp, dest, xb_buf):
    t, half = hp.shape
    tm = _pick(512, t)
    return pl.pallas_call(
        _dispatch_kernel,
        grid=(t // tm,),
        in_specs=[
            pl.BlockSpec((TOP_K, tm), lambda i: (0, i), memory_space=pltpu.SMEM),
            pl.BlockSpec((tm, half), lambda i: (i, 0)),
            pl.BlockSpec(memory_space=pl.ANY),
        ],
        out_specs=pl.BlockSpec(memory_space=pl.ANY),
        out_shape=jax.ShapeDtypeStruct(xb_buf.shape, xb_buf.dtype),
        input_output_aliases={2: 0},
        scratch_shapes=[pltpu.SemaphoreType.DMA],
        compiler_params=_dma_cparams(("arbitrary",)),
        name="dispatch",
    )(dest, hp, xb_buf)


def _expert_kernel(be_ref, nu_ref, nv_ref, xp_ref, wg_ref, bg_ref, wu_ref, bu_ref, wd_ref, bd_ref, o_ref, x_scr):
    i = pl.program_id(0)
    f = pl.program_id(1)
    half = xp_ref.shape[1]

    @pl.when(i >= nu_ref[0])
    def _():
        @pl.when(f == 0)
        def _():
            o_ref[...] = jnp.zeros(o_ref.shape, F32)

    @pl.when(i < nu_ref[0])
    def _():
        @pl.when(f == 0)
        def _():
            keep = lax.broadcasted_iota(I32, (xp_ref.shape[0], 1), 0) < nv_ref[i]
            lo, hi = _unpack_bf16_pairs(jnp.where(keep, xp_ref[...], jnp.uint32(0)))
            x_scr[:, :half] = lo
            x_scr[:, half:] = hi
            o_ref[...] = jnp.broadcast_to(bd_ref[0], o_ref.shape)

        x = x_scr[...]
        gate = jnp.minimum(_dot(x, wg_ref[...]) + bg_ref[0], SWIGLU_LIMIT)
        up = jnp.clip(_dot(x, wu_ref[...]) + bu_ref[0], -SWIGLU_LIMIT, SWIGLU_LIMIT)
        hdn = (up + 1.0) * (gate * jax.nn.sigmoid(SWIGLU_ALPHA * gate))
        o_ref[...] += _dot(hdn.astype(BF16), wd_ref[...].astype(BF16))


def _experts(xb, block_e, n_used, n_valid, layer, w_gate, b_gate, w_up, b_up, w_down, b_down, tm):
    cap, half = xb.shape
    d = 2 * half
    n_e, _, ff = w_gate.shape[1:]
    tf = 512
    nb = cap // tm
    cl = lambda i, nu: jnp.minimum(i, nu[0] - 1)
    wspec_in = pl.BlockSpec((None, None, d, tf), lambda i, f, be, nu, nv: (layer, be[cl(i, nu)], 0, f))
    wspec_out = pl.BlockSpec((None, None, tf, d), lambda i, f, be, nu, nv: (layer, be[cl(i, nu)], f, 0))
    bspec_in = pl.BlockSpec((1, 1, tf), lambda i, f, be, nu, nv: (layer * n_e + be[cl(i, nu)], 0, f))
    bspec_out = pl.BlockSpec((1, 1, d), lambda i, f, be, nu, nv: (layer * n_e + be[cl(i, nu)], 0, 0))
    grid_spec = pltpu.PrefetchScalarGridSpec(
        num_scalar_prefetch=3,
        grid=(nb, ff // tf),
        in_specs=[
            pl.BlockSpec((tm, half), lambda i, f, be, nu, nv: (cl(i, nu), 0)),
            wspec_in, bspec_in, wspec_in, bspec_in, wspec_out, bspec_out,
        ],
        out_specs=pl.BlockSpec((tm, d), lambda i, f, be, nu, nv: (i, 0)),
        scratch_shapes=[pltpu.VMEM((tm, d), BF16)],
    )
    return pl.pallas_call(
        _expert_kernel,
        grid_spec=grid_spec,
        out_shape=jax.ShapeDtypeStruct((cap, d), F32),
        compiler_params=_cparams(("parallel", "arbitrary")),
        name="experts",
    )(block_e, n_used, n_valid, xb, w_gate, b_gate.reshape(-1, 1, ff), w_up, b_up.reshape(-1, 1, ff),
      w_down, b_down.reshape(-1, 1, d))


def _combine_kernel(dcur_ref, dnxt_ref, w_ref, x_ref, mod_ref, yb_ref, o_ref, ybuf, sems, *, kg):
    i = pl.program_id(0)
    n = pl.num_programs(0)
    tm = x_ref.shape[0]
    slot = i % 2

    def fetch(d_ref, s):
        for t in range(tm):
            for k in range(TOP_K):
                pltpu.make_async_copy(yb_ref.at[pl.ds(d_ref[k, t], 1)], ybuf.at[s, k, pl.ds(t, 1)],
                                      sems.at[s]).start()

    @pl.when(i == 0)
    def _():
        fetch(dcur_ref, 0)

    for s in range(2):
        @pl.when(jnp.logical_and(i + 1 < n, slot == 1 - s))
        def _():
            fetch(dnxt_ref, s)

    for k in range(TOP_K):
        pltpu.make_async_copy(yb_ref.at[pl.ds(0, tm)], ybuf.at[slot, k], sems.at[slot]).wait()
    w = w_ref[...]
    f = ybuf[slot, 0] * w[:, 0:1]
    for k in range(1, TOP_K):
        f = f + ybuf[slot, k] * w[:, k:k + 1]
    o_ref[...] = x_ref[...] + mod_ref[0][kg:kg + 1, :] * f


def _combine(geo, yb, dest, w_tk, x, mod, kg):
    t, d = x.shape
    tm = _pick(128, geo.t_ctx, geo.seq_lat)
    nsteps = t // tm
    row = geo.mod_row(tm)
    return pl.pallas_call(
        functools.partial(_combine_kernel, kg=kg),
        grid=(nsteps,),
        in_specs=[
            pl.BlockSpec((TOP_K, tm), lambda i: (0, i), memory_space=pltpu.SMEM),
            pl.BlockSpec((TOP_K, tm), lambda i: (0, jnp.minimum(i + 1, nsteps - 1)), memory_space=pltpu.SMEM),
            pl.BlockSpec((tm, TOP_K), lambda i: (i, 0)),
            pl.BlockSpec((tm, d), lambda i: (i, 0)),
            pl.BlockSpec((1, 6, d), lambda i: (row(i), 0, 0)),
            pl.BlockSpec(memory_space=pl.ANY),
        ],
        out_specs=pl.BlockSpec((tm, d), lambda i: (i, 0)),
        out_shape=jax.ShapeDtypeStruct((t, d), F32),
        scratch_shapes=[pltpu.VMEM((2, TOP_K, tm, d), F32), pltpu.SemaphoreType.DMA((2,))],
        compiler_params=_dma_cparams(("arbitrary",)),
        name="combine",
    )(dest, dest, w_tk, x, mod, yb)


MOE_TM = 1024


def _moe_blocks(t, n_e):
    return -(-(t * TOP_K) // MOE_TM) + n_e


def _moe(geo, x, nw, mod, layer, w_router, b_router, w_gate, b_gate, w_up, b_up, w_down, b_down, xb_buf):
    t, d = x.shape
    n_e = w_router.shape[1]
    hp, top_e, top_w, rank, cnt = _router(geo, x, nw, mod, w_router.T, b_router, 3)
    counts = cnt[:, 0].astype(I32)
    padded = (counts + MOE_TM - 1) // MOE_TM * MOE_TM
    pad_end = jnp.cumsum(padded)
    pad_start = pad_end - padded
    nb = _moe_blocks(t, n_e)
    experts = jnp.arange(n_e, dtype=I32)
    start_of = jnp.sum(jnp.where(top_e[None] == experts[:, None, None], pad_start[:, None, None], 0), axis=0)
    dest = start_of + rank
    blk0 = jnp.arange(nb, dtype=I32) * MOE_TM
    block_e = jnp.minimum(jnp.sum((pad_end[None, :] <= blk0[:, None]).astype(I32), axis=1), n_e - 1)
    own = block_e[:, None] == experts[None, :]
    group_end = jnp.sum(jnp.where(own, (pad_start + counts)[None, :], 0), axis=1)
    n_valid = jnp.clip(group_end - blk0, 0, MOE_TM).astype(I32)
    n_used = (pad_end[-1:] // MOE_TM).astype(I32)
    xb = _dispatch(hp, dest, xb_buf)
    yb = _experts(xb, block_e, n_used, n_valid, layer, w_gate, b_gate, w_up, b_up, w_down, b_down, MOE_TM)
    return _combine(geo, yb, dest, top_w.T, x, mod, 5), xb


_SWAP = np.array([(j + ROPE_FREQ) if (j // ROPE_FREQ) % 2 == 0 else (j - ROPE_FREQ) for j in range(QK_ROPE)])


def _rope_table(seq_lat, tq):
    pos = jnp.arange(seq_lat)
    row = (pos // GRID_W).astype(F32)
    colp = (pos % GRID_W).astype(F32)
    inv = jnp.power(ROPE_THETA, -jnp.arange(ROPE_FREQ, dtype=F32) / ROPE_FREQ)
    ar, ac = row[:, None] * inv, colp[:, None] * inv
    cos = jnp.concatenate([jnp.cos(ar), jnp.cos(ar), jnp.cos(ac), jnp.cos(ac)], axis=1)
    sin = jnp.concatenate([-jnp.sin(ar), jnp.sin(ar), -jnp.sin(ac), jnp.sin(ac)], axis=1)
    ident = jnp.concatenate([jnp.ones((tq, QK_ROPE), F32), jnp.zeros((tq, QK_ROPE), F32)], axis=1)
    return jnp.concatenate([ident, jnp.concatenate([cos, sin], axis=1)], axis=0)


def _even_params(w_in, w_qb, q_norm, q_rope_norm, k_rope_norm):
    d = w_in.shape[0]
    o_kv, o_kr, o_conv = Q_LORA, Q_LORA + KV_LORA, Q_LORA + KV_LORA + QK_ROPE
    kr = w_in[:, o_kr:o_conv]
    pad = jnp.zeros((d, EV_N - EV_KR - 2 * QK_ROPE), w_in.dtype)
    w_in_r = jnp.concatenate([w_in[:, o_conv:], w_in[:, :o_kr], kr, kr[:, _SWAP], pad], axis=1).astype(BF16)
    wq = w_qb.reshape(Q_LORA, MLA_HEADS, QK_NOPE + QK_ROPE)
    wq_r = jnp.concatenate([wq, wq[:, :, QK_NOPE:][:, :, _SWAP]], axis=2).reshape(Q_LORA, MLA_HEADS * 256)
    q_nw = jnp.concatenate([q_norm, q_rope_norm, q_rope_norm[_SWAP]]).reshape(1, 256)
    kr_nw = jnp.concatenate([k_rope_norm, k_rope_norm[_SWAP]]).reshape(1, 2 * QK_ROPE)
    return w_in_r, wq_r.astype(BF16), q_nw, kr_nw


def _odd_params(w_in, w_gk, b_gk):
    d = w_in.shape[0]
    pad = jnp.zeros((d, OD_N - w_in.shape[1]), w_in.dtype)
    w_in_r = jnp.concatenate([w_in, pad], axis=1).astype(BF16)
    wpad = jnp.zeros((2, LANES, w_gk.shape[2]), F32)
    wpad = wpad.at[0, :GATE_RANK].set(w_gk[0]).at[1, GATE_RANK:2 * GATE_RANK].set(w_gk[1])
    return w_in_r, wpad.astype(BF16), b_gk.reshape(2, 1, -1)


def kernel(x_prompt, x_sample, cache_mla_ckv, cache_mla_krope, state_gla, c, c_ctx, ada_w, ada_b, norm_mix_w, norm_ffn_w, ev_w_in, ev_q_a_norm, ev_w_qb, ev_kv_a_norm, ev_w_kvb, ev_q_norm, ev_k_norm, ev_q_rope_norm, ev_k_rope_norm, ev_conv_w, ev_w_out, od_w_in, od_w_gk, od_b_gk, od_o_norm, od_w_out, moe_w_router, moe_b_router, moe_w_gate, moe_b_gate, moe_w_up, moe_b_up, moe_w_down, moe_b_down):
    geo = _Geo(x_prompt, x_sample, cache_mla_ckv)
    d = geo.d
    depth = ada_w.shape[0]
    x = jnp.concatenate([x_prompt.reshape(geo.t_ctx, d), x_sample.reshape(geo.t_lat, d)], axis=0)
    cond = jnp.concatenate([c_ctx[None, :], c, jnp.zeros((16 - 1 - geo.n_lat, d), F32)], axis=0)
    mods = _adaln(cond, ada_w, ada_b).reshape(depth, 16, 6, d)
    tq = _pick(512, geo.t_ctx, geo.seq_lat)
    tab = _rope_table(geo.seq_lat, tq)
    w_gate, w_up, w_down = moe_w_gate.astype(BF16), moe_w_up.astype(BF16), moe_w_down
    xb_buf = jnp.zeros((_moe_blocks(geo.t, moe_w_router.shape[2]) * MOE_TM, d // 2), U32)
    attn_buf = jnp.zeros((geo.t, MLA_HEADS * V_HEAD), BF16)
    gla_bufs = (jnp.zeros((geo.t, GLA_HEADS * GLA_DVH), BF16), jnp.zeros((geo.t, GLA_HEADS * GLA_DVH), BF16))
    new_ckv, new_krope, new_gla = [], [], []
    for l in range(depth):
        i = l // 2
        mod = mods[l]
        if l % 2 == 0:
            w_in_r, wq_r, q_nw, kr_nw = _even_params(ev_w_in[i], ev_w_qb[i], ev_q_norm[i], ev_q_rope_norm[i],
                                                     ev_k_rope_norm[i])
            z = _lnmod_mm(geo, x, norm_mix_w[l], mod, w_in_r, 0)
            conv, qan, ckv, krn, krot = _even_mid(geo, z, ev_conv_w[i], ev_q_a_norm[i], ev_kv_a_norm[i], kr_nw,
                                                  tab, tq)
            q256 = _q_proj(geo, qan, wq_r, q_nw, tab, tq)
            kc = cache_mla_krope[:, i].reshape(geo.n_lat * geo.past, QK_ROPE)
            ckv_all = jnp.concatenate([ckv, cache_mla_ckv[:, i].reshape(geo.n_lat * geo.past, KV_LORA)], axis=0)
            krot_all = jnp.concatenate([krot, jnp.concatenate([kc, kc], axis=1).astype(BF16)], axis=0)
            k256, v256 = _kv_proj(ckv_all, ev_w_kvb[i].astype(BF16), ev_k_norm[i], krot_all)
            attn_buf = _attention(geo, q256, k256, v256, attn_buf)
            x = _mix_out(geo, attn_buf, conv, ev_w_out[i].astype(BF16), x, mod, 2)
            new_ckv.append(ckv[:geo.t_ctx].reshape(geo.n_ctx, geo.seq_ctx, KV_LORA))
            new_krope.append(krn[:geo.t_ctx].reshape(geo.n_ctx, geo.seq_ctx, QK_ROPE))
        else:
            w_in_r, wgk_pad, bgk = _odd_params(od_w_in[i], od_w_gk[i], od_b_gk[i])
            z = _lnmod_mm(geo, x, norm_mix_w[l], mod, w_in_r, 0)
            o_f, o_b, s_fin = _gla_scan(z, wgk_pad, bgk, geo.n_ctx, geo.seq_ctx, 0, None, True, gla_bufs)
            gla_bufs = _gla_scan(z, wgk_pad, bgk, geo.n_lat, geo.seq_lat, geo.t_ctx, state_gla[:, i], False,
                                 (o_f, o_b))
            x = _gla_out(geo, gla_bufs[0], gla_bufs[1], z, od_o_norm[i], od_w_out[i].astype(BF16), x, mod, 2)
            new_gla.append(s_fin)
        x, xb_buf = _moe(geo, x, norm_ffn_w[l], mod, l, moe_w_router[l], moe_b_router[l], w_gate, moe_b_gate,
                         w_up, moe_b_up, w_down, moe_b_down, xb_buf)
    xp = x[:geo.t_ctx].reshape(x_prompt.shape)
    xs = x[geo.t_ctx:].reshape(x_sample.shape)
    return (xp, xs, jnp.stack(new_ckv, axis=1), jnp.stack(new_krope, axis=1), jnp.stack(new_gla, axis=1))
```

```python
import functools

import jax
import jax.numpy as jnp
import numpy as np
from jax import lax
from jax.experimental import pallas as pl
from jax.experimental.pallas import tpu as pltpu

F32 = jnp.float32
BF16 = jnp.bfloat16
I32 = jnp.int32
U32 = jnp.uint32

EPS = 1e-6
GRID_W = 64
MLA_HEADS = 8
QK_NOPE = 128
QK_ROPE = 64
V_HEAD = 128
Q_LORA = 512
KV_LORA = 256
ROPE_THETA = 10000.0
ROPE_FREQ = QK_ROPE // 4
CONV_W = 1024
GLA_HEADS = 4
GLA_DKH = 256
GLA_DVH = 512
GATE_RANK = 16
GATE_NORM = 16.0
GLA_CHUNK = 64
TOP_K = 4
SWIGLU_ALPHA = 1.702
SWIGLU_LIMIT = 7.0

V7X_VMEM_LIMIT = 56 * 1024 * 1024
LANES = 128
HI = lax.Precision.HIGHEST


def _cparams(sem):
    return pltpu.CompilerParams(dimension_semantics=sem, vmem_limit_bytes=V7X_VMEM_LIMIT)


def _pick(want, *dims):
    t = want
    while any(d % t for d in dims):
        t //= 2
    return t


def _dot(a, b):
    return jnp.dot(a, b, preferred_element_type=F32)


def _rms(x, w):
    return x * lax.rsqrt(jnp.mean(x * x, axis=-1, keepdims=True) + EPS) * w


def _lnmod(x, nw, mod, ks):
    gain = nw * (1.0 + mod[ks + 1:ks + 2, :])
    return x * lax.rsqrt(jnp.mean(x * x, axis=-1, keepdims=True) + EPS) * gain + mod[ks:ks + 1, :]


class _Geo:
    def __init__(self, x_prompt, x_sample, cache_ckv):
        self.n_ctx, self.seq_ctx, self.d = x_prompt.shape
        self.n_lat, self.seq_lat, _ = x_sample.shape
        self.past = cache_ckv.shape[2]
        self.t_ctx = self.n_ctx * self.seq_ctx
        self.t_lat = self.n_lat * self.seq_lat
        self.t = self.t_ctx + self.t_lat

    def mod_row(self, tm):
        t_ctx, seq_lat = self.t_ctx, self.seq_lat
        return lambda i: jnp.where(i * tm < t_ctx, 0, 1 + (i * tm - t_ctx) // seq_lat)


def _adaln_kernel(c_ref, w_ref, b_ref, o_ref):
    c = c_ref[...]
    a = (c * jax.nn.sigmoid(c)).astype(BF16)
    o_ref[0] = _dot(a, w_ref[0].astype(BF16)) + b_ref[0]


def _adaln(cond16, ada_w, ada_b):
    depth, d, n = ada_w.shape
    tn = 1024
    return pl.pallas_call(
        _adaln_kernel,
        grid=(depth, n // tn),
        in_specs=[
            pl.BlockSpec((16, d), lambda l, j: (0, 0)),
            pl.BlockSpec((1, d, tn), lambda l, j: (l, 0, j)),
            pl.BlockSpec((1, 1, tn), lambda l, j: (l, 0, j)),
        ],
        out_specs=pl.BlockSpec((1, 16, tn), lambda l, j: (l, 0, j)),
        out_shape=jax.ShapeDtypeStruct((depth, 16, n), F32),
        compiler_params=_cparams(("parallel", "parallel")),
        name="adaln",
    )(cond16, ada_w, ada_b.reshape(depth, 1, n))


def _lnmod_mm_kernel(x_ref, nw_ref, mod_ref, w_ref, o_ref, h_scr, *, ks):
    @pl.when(pl.program_id(1) == 0)
    def _():
        h_scr[...] = _lnmod(x_ref[...], nw_ref[...], mod_ref[0], ks).astype(BF16)

    o_ref[...] = _dot(h_scr[...], w_ref[...])


def _lnmod_mm(geo, x, nw, mod, w, ks):
    t, d = x.shape
    n = w.shape[1]
    tm = _pick(1024, geo.t_ctx, geo.seq_lat)
    tn = max(c for c in range(LANES, 1024 + 1, LANES) if n % c == 0)
    row = geo.mod_row(tm)
    return pl.pallas_call(
        functools.partial(_lnmod_mm_kernel, ks=ks),
        grid=(t // tm, n // tn),
        in_specs=[
            pl.BlockSpec((tm, d), lambda i, j: (i, 0)),
            pl.BlockSpec((1, d), lambda i, j: (0, 0)),
            pl.BlockSpec((1, 6, d), lambda i, j: (row(i), 0, 0)),
            pl.BlockSpec((d, tn), lambda i, j: (0, j)),
        ],
        out_specs=pl.BlockSpec((tm, tn), lambda i, j: (i, j)),
        out_shape=jax.ShapeDtypeStruct((t, n), F32),
        scratch_shapes=[pltpu.VMEM((tm, d), BF16)],
        compiler_params=_cparams(("parallel", "arbitrary")),
        name="lnmod_mm",
    )(x, nw.reshape(1, d), mod, w)


EV_QA = 3 * CONV_W
EV_KVA = EV_QA + Q_LORA
EV_KR = EV_KVA + KV_LORA
EV_N = 4096


def _even_mid_kernel(bg_ref, cg_ref, xv_ref, cgp_ref, xvp_ref, cgn_ref, xvn_ref, qa_ref, kva_ref, kr_ref,
                     cw_ref, qaw_ref, kvw_ref, krw_ref, tab_ref,
                     conv_ref, qan_ref, ckv_ref, krn_ref, krot_ref, *, t_ctx, seq_ctx, seq_lat):
    i = pl.program_id(0)
    tm = bg_ref.shape[0]
    u = cg_ref[...] * xv_ref[...]
    rows = lax.broadcasted_iota(I32, (tm, 1), 0)
    tok = i * tm + rows
    pos = jnp.where(tok < t_ctx, tok & (seq_ctx - 1), (tok - t_ctx) & (seq_lat - 1))
    last = jnp.where(tok < t_ctx, seq_ctx - 1, seq_lat - 1)
    u_m1 = jnp.where(rows == 0, cgp_ref[7:8, :] * xvp_ref[7:8, :], pltpu.roll(u, 1, axis=0))
    u_p1 = jnp.where(rows == tm - 1, cgn_ref[0:1, :] * xvn_ref[0:1, :], pltpu.roll(u, tm - 1, axis=0))
    u_m1 = jnp.where(pos == 0, 0.0, u_m1)
    u_p1 = jnp.where(pos == last, 0.0, u_p1)
    cw = cw_ref[...]
    conv = bg_ref[...] * (u_m1 * cw[0:1, :] + u * cw[1:2, :] + u_p1 * cw[2:3, :])
    conv_ref[...] = conv.astype(BF16)
    qan_ref[...] = _rms(qa_ref[...], qaw_ref[...]).astype(BF16)
    ckv_ref[...] = _rms(kva_ref[...], kvw_ref[...])
    kr = kr_ref[...]
    krn = _rms(kr, krw_ref[...])
    krn_ref[...] = krn[:, :QK_ROPE]
    y = krn * tab_ref[...]
    krot_ref[...] = (y + pltpu.roll(y, QK_ROPE, axis=1)).astype(BF16)


def _even_mid(geo, z, conv_w, qa_w, kv_w, kr_w128, tab, tq):
    t = z.shape[0]
    tm = tq
    ncb = geo.t_ctx // tm
    bps = geo.seq_lat // tm
    nb8 = t // 8
    r8 = tm // 8
    assert geo.seq_ctx & (geo.seq_ctx - 1) == 0 and geo.seq_lat & (geo.seq_lat - 1) == 0
    tab_idx = lambda i: jnp.where(i < ncb, 0, 1 + (i - ncb) % bps)
    col = lambda c: (lambda i: (i, c))
    in_specs = [
        pl.BlockSpec((tm, CONV_W), col(0)),
        pl.BlockSpec((tm, CONV_W), col(1)),
        pl.BlockSpec((tm, CONV_W), col(2)),
        pl.BlockSpec((8, CONV_W), lambda i: (jnp.maximum(i * r8 - 1, 0), 1)),
        pl.BlockSpec((8, CONV_W), lambda i: (jnp.maximum(i * r8 - 1, 0), 2)),
        pl.BlockSpec((8, CONV_W), lambda i: (jnp.minimum((i + 1) * r8, nb8 - 1), 1)),
        pl.BlockSpec((8, CONV_W), lambda i: (jnp.minimum((i + 1) * r8, nb8 - 1), 2)),
        pl.BlockSpec((tm, Q_LORA), col(EV_QA // Q_LORA)),
        pl.BlockSpec((tm, KV_LORA), col(EV_KVA // KV_LORA)),
        pl.BlockSpec((tm, 2 * QK_ROPE), col(EV_KR // (2 * QK_ROPE))),
        pl.BlockSpec((3, CONV_W), lambda i: (0, 0)),
        pl.BlockSpec((1, Q_LORA), lambda i: (0, 0)),
        pl.BlockSpec((1, KV_LORA), lambda i: (0, 0)),
        pl.BlockSpec((1, 2 * QK_ROPE), lambda i: (0, 0)),
        pl.BlockSpec((tm, 2 * QK_ROPE), lambda i: (tab_idx(i), 0)),
    ]
    out_specs = [
        pl.BlockSpec((tm, CONV_W), lambda i: (i, 0)),
        pl.BlockSpec((tm, Q_LORA), lambda i: (i, 0)),
        pl.BlockSpec((tm, KV_LORA), lambda i: (i, 0)),
        pl.BlockSpec((tm, QK_ROPE), lambda i: (i, 0)),
        pl.BlockSpec((tm, 2 * QK_ROPE), lambda i: (i, 0)),
    ]
    out_shape = [
        jax.ShapeDtypeStruct((t, CONV_W), BF16),
        jax.ShapeDtypeStruct((t, Q_LORA), BF16),
        jax.ShapeDtypeStruct((t, KV_LORA), F32),
        jax.ShapeDtypeStruct((t, QK_ROPE), F32),
        jax.ShapeDtypeStruct((t, 2 * QK_ROPE), BF16),
    ]
    return pl.pallas_call(
        functools.partial(_even_mid_kernel, t_ctx=geo.t_ctx, seq_ctx=geo.seq_ctx, seq_lat=geo.seq_lat),
        grid=(t // tm,),
        in_specs=in_specs,
        out_specs=out_specs,
        out_shape=out_shape,
        compiler_params=_cparams(("parallel",)),
        name="even_mid",
    )(z, z, z, z, z, z, z, z, z, z, conv_w, qa_w.reshape(1, -1), kv_w.reshape(1, -1), kr_w128, tab)


def _q_proj_kernel(a_ref, w_ref, nw_ref, tab_ref, o_ref):
    acc = _dot(a_ref[...], w_ref[...])
    nw = nw_ref[...]
    tab = tab_ref[...]
    for h in range(MLA_HEADS):
        n = acc[:, h * 256:h * 256 + QK_NOPE]
        r = acc[:, h * 256 + QK_NOPE:(h + 1) * 256]
        qn = _rms(n, nw[:, :QK_NOPE])
        y = _rms(r, nw[:, QK_NOPE:]) * tab
        o_ref[h] = (jnp.concatenate([qn, y], axis=-1) * Q_PRESCALE).astype(BF16)


def _q_proj(geo, qan, w_qb, nw256, tab, tq):
    t = qan.shape[0]
    tm = tq
    ncb = geo.t_ctx // tm
    bps = geo.seq_lat // tm
    n = w_qb.shape[1]
    return pl.pallas_call(
        _q_proj_kernel,
        grid=(t // tm,),
        in_specs=[
            pl.BlockSpec((tm, Q_LORA), lambda i: (i, 0)),
            pl.BlockSpec((Q_LORA, n), lambda i: (0, 0)),
            pl.BlockSpec((1, 256), lambda i: (0, 0)),
            pl.BlockSpec((tm, 2 * QK_ROPE), lambda i: (jnp.where(i < ncb, 0, 1 + (i - ncb) % bps), 0)),
        ],
        out_specs=pl.BlockSpec((MLA_HEADS, tm, 256), lambda i: (0, i, 0)),
        out_shape=jax.ShapeDtypeStruct((MLA_HEADS, t, 256), BF16),
        compiler_params=_cparams(("parallel",)),
        name="q_proj",
    )(qan, w_qb, nw256, tab)


def _kv_proj_kernel(a_ref, w_ref, nw_ref, kr_ref, k_ref, v_ref):
    acc = _dot(a_ref[...].astype(BF16), w_ref[...])
    kr = kr_ref[...]
    ones = jnp.ones((acc.shape[0], V_HEAD), BF16)
    for h in range(MLA_HEADS):
        kn = _rms(acc[:, h * 256:h * 256 + QK_NOPE], nw_ref[...]).astype(BF16)
        k_ref[h] = jnp.concatenate([kn, kr], axis=-1)
        v_ref[h] = jnp.concatenate([acc[:, h * 256 + QK_NOPE:(h + 1) * 256].astype(BF16), ones], axis=-1)


def _kv_proj(ckv_all, w_kvb, k_norm, krot_all):
    t = ckv_all.shape[0]
    tm = _pick(512, t)
    n = w_kvb.shape[1]
    return pl.pallas_call(
        _kv_proj_kernel,
        grid=(t // tm,),
        in_specs=[
            pl.BlockSpec((tm, KV_LORA), lambda i: (i, 0)),
            pl.BlockSpec((KV_LORA, n), lambda i: (0, 0)),
            pl.BlockSpec((1, QK_NOPE), lambda i: (0, 0)),
            pl.BlockSpec((tm, 2 * QK_ROPE), lambda i: (i, 0)),
        ],
        out_specs=[
            pl.BlockSpec((MLA_HEADS, tm, 256), lambda i: (0, i, 0)),
            pl.BlockSpec((MLA_HEADS, tm, 2 * V_HEAD), lambda i: (0, i, 0)),
        ],
        out_shape=[
            jax.ShapeDtypeStruct((MLA_HEADS, t, 256), BF16),
            jax.ShapeDtypeStruct((MLA_HEADS, t, 2 * V_HEAD), BF16),
        ],
        compiler_params=_cparams(("parallel",)),
        name="kv_proj",
    )(ckv_all, w_kvb, k_norm.reshape(1, -1), krot_all)


_NT = (((1,), (1,)), ((), ()))
_TN = (((0,), (0,)), ((), ()))
Q_PRESCALE = (QK_NOPE + QK_ROPE) ** -0.5 * 1.4426950408889634
ATTN_KEY_CHUNK = 512


def _softmax_step(q, k, v, m, acc):
    s = lax.dot_general(q, k, _NT, preferred_element_type=F32)
    m_new = jnp.maximum(m, jnp.max(s, axis=-1, keepdims=True))
    p = jnp.exp2(s - m_new)
    acc = acc * jnp.exp2(m - m_new) + _dot(p.astype(BF16), v)
    return m_new, acc


def _attn_ctx_kernel(q_ref, k_ref, v_ref, prev_ref, o_ref):
    del prev_ref
    outs = []
    for h in range(MLA_HEADS):
        s = lax.dot_general(q_ref[h], k_ref[h], _NT, preferred_element_type=F32)
        p = jnp.exp2(s - jnp.max(s, axis=-1, keepdims=True))
        o = _dot(p.astype(BF16), v_ref[h])
        outs.append(o[:, :V_HEAD] / o[:, V_HEAD:V_HEAD + 1])
    o_ref[...] = jnp.concatenate(outs, axis=-1).astype(BF16)


def _attn_lat_kernel(q_ref, kc_ref, vc_ref, kl_ref, vl_ref, prev_ref, o_ref):
    del prev_ref
    q = q_ref[0]
    tq = q.shape[0]
    m = jnp.full((tq, 1), -jnp.inf, F32)
    acc = jnp.zeros((tq, 2 * V_HEAD), F32)
    for k_ref, v_ref in ((kc_ref, vc_ref), (kl_ref, vl_ref)):
        n = k_ref.shape[1]
        ck = min(ATTN_KEY_CHUNK, n)
        for j in range(n // ck):
            m, acc = _softmax_step(q, k_ref[0, j * ck:(j + 1) * ck, :], v_ref[0, j * ck:(j + 1) * ck, :], m, acc)
    o_ref[...] = (acc[:, :V_HEAD] / acc[:, V_HEAD:V_HEAD + 1]).astype(BF16)


def _attention(geo, q256, k256, v256, o_buf):
    t = geo.t
    sc, sl, past = geo.seq_ctx, geo.seq_lat, geo.past
    hv = MLA_HEADS * V_HEAD
    vw = 2 * V_HEAD
    o_ctx = pl.pallas_call(
        _attn_ctx_kernel,
        grid=(geo.n_ctx,),
        in_specs=[
            pl.BlockSpec((MLA_HEADS, sc, 256), lambda b: (0, b, 0)),
            pl.BlockSpec((MLA_HEADS, sc, 256), lambda b: (0, b, 0)),
            pl.BlockSpec((MLA_HEADS, sc, vw), lambda b: (0, b, 0)),
            pl.BlockSpec(memory_space=pl.ANY),
        ],
        out_specs=pl.BlockSpec((sc, hv), lambda b: (b, 0)),
        out_shape=jax.ShapeDtypeStruct((t, hv), BF16),
        input_output_aliases={3: 0},
        compiler_params=_cparams(("parallel",)),
        name="attn_ctx",
    )(q256, k256, v256, o_buf)
    tq = _pick(512, sl)
    nq = sl // tq
    q_off = geo.t_ctx // tq
    kl_off = geo.t_ctx // sl
    kc_off = t // past
    return pl.pallas_call(
        _attn_lat_kernel,
        grid=(geo.n_lat, MLA_HEADS, nq),
        in_specs=[
            pl.BlockSpec((1, tq, 256), lambda b, h, i: (h, q_off + b * nq + i, 0)),
            pl.BlockSpec((1, past, 256), lambda b, h, i: (h, kc_off + b, 0)),
            pl.BlockSpec((1, past, vw), lambda b, h, i: (h, kc_off + b, 0)),
            pl.BlockSpec((1, sl, 256), lambda b, h, i: (h, kl_off + b, 0)),
            pl.BlockSpec((1, sl, vw), lambda b, h, i: (h, kl_off + b, 0)),
            pl.BlockSpec(memory_space=pl.ANY),
        ],
        out_specs=pl.BlockSpec((tq, V_HEAD), lambda b, h, i: (q_off + b * nq + i, h)),
        out_shape=jax.ShapeDtypeStruct((t, hv), BF16),
        input_output_aliases={5: 0},
        compiler_params=_cparams(("parallel", "parallel", "arbitrary")),
        name="attn_lat",
    )(q256, k256, v256, k256, v256, o_ctx)


def _mix_out_kernel(a1_ref, a2_ref, w_ref, x_ref, mod_ref, o_ref, *, kg):
    k1 = a1_ref.shape[1]
    acc = _dot(a1_ref[...], w_ref[:k1, :]) + _dot(a2_ref[...], w_ref[k1:, :])
    o_ref[...] = x_ref[...] + mod_ref[0][kg:kg + 1, :] * acc


def _mix_out(geo, a1, a2, w, x, mod, kg):
    t, d = x.shape
    tm = _pick(512, geo.t_ctx, geo.seq_lat)
    row = geo.mod_row(tm)
    k1, k2 = a1.shape[1], a2.shape[1]
    return pl.pallas_call(
        functools.partial(_mix_out_kernel, kg=kg),
        grid=(t // tm,),
        in_specs=[
            pl.BlockSpec((tm, k1), lambda i: (i, 0)),
            pl.BlockSpec((tm, k2), lambda i: (i, 0)),
            pl.BlockSpec((k1 + k2, d), lambda i: (0, 0)),
            pl.BlockSpec((tm, d), lambda i: (i, 0)),
            pl.BlockSpec((1, 6, d), lambda i: (row(i), 0, 0)),
        ],
        out_specs=pl.BlockSpec((tm, d), lambda i: (i, 0)),
        out_shape=jax.ShapeDtypeStruct((t, d), F32),
        compiler_params=_cparams(("parallel",)),
        name="mix_out",
    )(a1, a2, w, x, mod)


OD_K = GLA_HEADS * GLA_DKH
OD_V = 2 * OD_K
OD_G = OD_V + GLA_HEADS * GLA_DVH
OD_R = OD_G + GLA_HEADS * GLA_DVH
OD_N = OD_R + LANES


def _log_sigmoid(x):
    return jnp.minimum(x, 0.0) - jnp.log1p(jnp.exp(-jnp.abs(x)))


def _split3(x):
    hi = x.astype(BF16)
    r1 = x - hi.astype(F32)
    mid = r1.astype(BF16)
    lo = (r1 - mid.astype(F32)).astype(BF16)
    return hi, mid, lo


def _gla_chunk(q, k, v, r, wgk, bgk, st_ref, o_ref, backward):
    c = q.shape[0]
    la = _log_sigmoid(_dot(r.astype(BF16), wgk) + bgk) * (1.0 / GATE_NORM)
    ii = lax.broadcasted_iota(I32, (c, c), 0)
    jj = lax.broadcasted_iota(I32, (c, c), 1)
    causal = (ii <= jj) if backward else (ii >= jj)
    tri = causal.astype(BF16)
    hi, mid, lo = _split3(la)
    b = _dot(tri, hi) + _dot(tri, mid) + _dot(tri, lo)
    tot = jnp.sum(la, axis=0, keepdims=True)
    q_dec = (q * (GLA_DKH ** -0.5) * jnp.exp(b)).astype(BF16)
    k_dec = (k * jnp.exp(-b)).astype(BF16)
    k_end = (k * jnp.exp(tot - b)).astype(BF16)
    e_tot = jnp.exp(tot)
    vb = v.astype(BF16)
    heads = range(GLA_HEADS)
    ks = [slice(h * GLA_DKH, (h + 1) * GLA_DKH) for h in heads]
    vs = [slice(h * GLA_DVH, (h + 1) * GLA_DVH) for h in heads]
    scores = [jnp.where(causal, lax.dot_general(q_dec[:, ks[h]], k_dec[:, ks[h]], _NT, preferred_element_type=F32),
                        0.0).astype(BF16) for h in heads]
    st = [st_ref[h] for h in heads]
    for h in heads:
        o_ref[:, vs[h]] = (_dot(scores[h], vb[:, vs[h]]) + lax.dot_general(
            q_dec[:, ks[h]], st[h].astype(BF16), _NT, preferred_element_type=F32)).astype(o_ref.dtype)
    for h in heads:
        st_ref[h] = e_tot[:, ks[h]] * st[h] + lax.dot_general(vb[:, vs[h]], k_end[:, ks[h]], _TN,
                                                             preferred_element_type=F32)


def _gla_kernel(*refs, has_init, want_final):
    qf, kf, vf, rf, qb, kb, vb, rb, wgk_ref, bgk_ref = refs[:10]
    pos = 10
    s0_ref = None
    if has_init:
        s0_ref = refs[pos]
        pos += 1
    pos += 2
    of_ref, ob_ref = refs[pos], refs[pos + 1]
    pos += 2
    sfin_ref = None
    if want_final:
        sfin_ref = refs[pos]
        pos += 1
    st_scr = refs[pos]
    c = pl.program_id(1)
    nc = pl.num_programs(1)

    @pl.when(c == 0)
    def _():
        for d in range(2):
            for h in range(GLA_HEADS):
                if has_init:
                    st_scr[d, h] = s0_ref[0, d, h].T
                else:
                    st_scr[d, h] = jnp.zeros(st_scr.shape[2:], F32)

    _gla_chunk(qf[...], kf[...], vf[...], rf[...], wgk_ref[0], bgk_ref[0], st_scr.at[0], of_ref, False)
    _gla_chunk(qb[...], kb[...], vb[...], rb[...], wgk_ref[1], bgk_ref[1], st_scr.at[1], ob_ref, True)

    if want_final:
        @pl.when(c == nc - 1)
        def _():
            for d in range(2):
                for h in range(GLA_HEADS):
                    sfin_ref[0, d, h] = st_scr[d, h].T


def _gla_scan(z, wgk_pad, bgk, n_seq, seq, row0, s0, want_final, bufs):
    t = z.shape[0]
    c = GLA_CHUNK
    nc = seq // c
    base = row0 // c
    fwd = lambda b, i: base + b * nc + i
    bwd = lambda b, i: base + b * nc + (nc - 1 - i)
    dk, dv = GLA_HEADS * GLA_DKH, GLA_HEADS * GLA_DVH

    def zspecs(rowf):
        return [
            pl.BlockSpec((c, dk), lambda b, i: (rowf(b, i), 0)),
            pl.BlockSpec((c, dk), lambda b, i: (rowf(b, i), OD_K // dk)),
            pl.BlockSpec((c, dv), lambda b, i: (rowf(b, i), OD_V // dv)),
            pl.BlockSpec((c, LANES), lambda b, i: (rowf(b, i), OD_R // LANES)),
        ]

    in_specs = zspecs(fwd) + zspecs(bwd) + [
        pl.BlockSpec((2, LANES, dk), lambda b, i: (0, 0, 0)),
        pl.BlockSpec((2, 1, dk), lambda b, i: (0, 0, 0)),
    ]
    args = [z] * 8 + [wgk_pad, bgk]
    if s0 is not None:
        in_specs.append(pl.BlockSpec((1, 2, GLA_HEADS, GLA_DKH, GLA_DVH), lambda b, i: (b, 0, 0, 0, 0)))
        args.append(s0)
    aliases = {len(args): 0, len(args) + 1: 1}
    in_specs += [pl.BlockSpec(memory_space=pl.ANY)] * 2
    args += list(bufs)
    out_specs = [pl.BlockSpec((c, dv), lambda b, i: (fwd(b, i), 0)), pl.BlockSpec((c, dv), lambda b, i: (bwd(b, i), 0))]
    out_shape = [jax.ShapeDtypeStruct((t, dv), bufs[0].dtype)] * 2
    if want_final:
        out_specs.append(pl.BlockSpec((1, 2, GLA_HEADS, GLA_DKH, GLA_DVH), lambda b, i: (b, 0, 0, 0, 0)))
        out_shape.append(jax.ShapeDtypeStruct((n_seq, 2, GLA_HEADS, GLA_DKH, GLA_DVH), F32))
    return pl.pallas_call(
        functools.partial(_gla_kernel, has_init=s0 is not None, want_final=want_final),
        grid=(n_seq, nc),
        in_specs=in_specs,
        out_specs=out_specs,
        out_shape=out_shape,
        input_output_aliases=aliases,
        scratch_shapes=[pltpu.VMEM((2, GLA_HEADS, GLA_DVH, GLA_DKH), F32)],
        compiler_params=_cparams(("parallel", "arbitrary")),
        name="gla_scan",
    )(*args)


def _gla_out_kernel(of_ref, ob_ref, g_ref, nw_ref, w_ref, x_ref, mod_ref, o_ref, a_scr, *, kg):
    @pl.when(pl.program_id(1) == 0)
    def _():
        nw = nw_ref[...]
        for h in range(GLA_HEADS):
            sl = slice(h * GLA_DVH, (h + 1) * GLA_DVH)
            o = _rms(of_ref[:, sl].astype(F32) + ob_ref[:, sl].astype(F32), nw)
            g = g_ref[:, sl]
            a_scr[:, sl] = (o * (g * jax.nn.sigmoid(g))).astype(BF16)

    o_ref[...] = x_ref[...] + mod_ref[0][kg:kg + 1, :] * _dot(a_scr[...], w_ref[...])


def _gla_out(geo, o_f, o_b, z, o_norm, w_out, x, mod, kg):
    t, d = x.shape
    dv = GLA_HEADS * GLA_DVH
    tm = _pick(512, geo.t_ctx, geo.seq_lat)
    tn = d
    row = geo.mod_row(tm)
    return pl.pallas_call(
        functools.partial(_gla_out_kernel, kg=kg),
        grid=(t // tm, d // tn),
        in_specs=[
            pl.BlockSpec((tm, dv), lambda i, j: (i, 0)),
            pl.BlockSpec((tm, dv), lambda i, j: (i, 0)),
            pl.BlockSpec((tm, dv), lambda i, j: (i, OD_G // dv)),
            pl.BlockSpec((1, GLA_DVH), lambda i, j: (0, 0)),
            pl.BlockSpec((dv, tn), lambda i, j: (0, j)),
            pl.BlockSpec((tm, tn), lambda i, j: (i, j)),
            pl.BlockSpec((1, 6, tn), lambda i, j: (row(i), 0, j)),
        ],
        out_specs=pl.BlockSpec((tm, tn), lambda i, j: (i, j)),
        out_shape=jax.ShapeDtypeStruct((t, d), F32),
        scratch_shapes=[pltpu.VMEM((tm, dv), BF16)],
        compiler_params=_cparams(("parallel", "arbitrary")),
        name="gla_out",
    )(o_f, o_b, z, o_norm.reshape(1, -1), w_out, x, mod)


def _pack_bf16_pairs(h):
    n = h.shape[1] // 2
    lo = pltpu.bitcast(h[:, :n].astype(BF16).astype(F32), U32)
    hi = pltpu.bitcast(h[:, n:].astype(BF16).astype(F32), U32)
    return (lo >> 16) | (hi & jnp.uint32(0xFFFF0000))


def _unpack_bf16_pairs(w):
    lo = pltpu.bitcast(w << 16, F32).astype(BF16)
    hi = pltpu.bitcast(w & jnp.uint32(0xFFFF0000), F32).astype(BF16)
    return lo, hi


def _router_kernel(x_ref, nw_ref, mod_ref, wr_ref, br_ref, tri_ref,
                   hp_ref, e_ref, w_ref, rank_ref, cnt_ref, carry_scr, *, ks):
    @pl.when(pl.program_id(0) == 0)
    def _():
        carry_scr[...] = jnp.zeros(carry_scr.shape, F32)

    h = _lnmod(x_ref[...], nw_ref[...], mod_ref[0], ks)
    hp_ref[...] = _pack_bf16_pairs(h)
    lg = lax.dot_general(wr_ref[...], h, _NT, precision=HI, preferred_element_type=F32) + br_ref[...]
    n_e, tm = lg.shape
    rows = lax.broadcasted_iota(I32, (n_e, tm), 0).astype(F32)
    vals, sels, hits = [], [], []
    for _ in range(TOP_K):
        m = jnp.max(lg, axis=0, keepdims=True)
        idx = jnp.min(jnp.where(lg == m, rows, float(n_e)), axis=0, keepdims=True)
        hit = rows == idx
        vals.append(m)
        sels.append(idx)
        hits.append(hit)
        lg = jnp.where(hit, -jnp.inf, lg)
    ex = [jnp.exp(v - vals[0]) for v in vals]
    den = ex[0] + ex[1] + ex[2] + ex[3]
    w_ref[...] = jnp.concatenate([e / den for e in ex], axis=0)
    e_ref[...] = jnp.concatenate(sels, axis=0).astype(I32)
    onehot = jnp.zeros((n_e, tm), F32)
    for hit in hits:
        onehot = jnp.where(hit, 1.0, onehot)
    before = carry_scr[:, 0:1] + _dot(onehot.astype(BF16), tri_ref[...])
    rank_ref[...] = jnp.concatenate(
        [jnp.sum(jnp.where(hit, before, 0.0), axis=0, keepdims=True) for hit in hits], axis=0).astype(I32)
    carry_scr[...] = carry_scr[...] + jnp.sum(onehot, axis=1, keepdims=True)
    cnt_ref[...] = carry_scr[...]


def _router(geo, x, nw, mod, w_router_t, b_router, ks):
    t, d = x.shape
    n_e = w_router_t.shape[0]
    tm = _pick(512, geo.t_ctx, geo.seq_lat)
    row = geo.mod_row(tm)
    tri = (lax.broadcasted_iota(I32, (tm, tm), 0) < lax.broadcasted_iota(I32, (tm, tm), 1)).astype(BF16)
    return pl.pallas_call(
        functools.partial(_router_kernel, ks=ks),
        grid=(t // tm,),
        in_specs=[
            pl.BlockSpec((tm, d), lambda i: (i, 0)),
            pl.BlockSpec((1, d), lambda i: (0, 0)),
            pl.BlockSpec((1, 6, d), lambda i: (row(i), 0, 0)),
            pl.BlockSpec((n_e, d), lambda i: (0, 0)),
            pl.BlockSpec((n_e, 1), lambda i: (0, 0)),
            pl.BlockSpec((tm, tm), lambda i: (0, 0)),
        ],
        out_specs=[
            pl.BlockSpec((tm, d // 2), lambda i: (i, 0)),
            pl.BlockSpec((TOP_K, tm), lambda i: (0, i)),
            pl.BlockSpec((TOP_K, tm), lambda i: (0, i)),
            pl.BlockSpec((TOP_K, tm), lambda i: (0, i)),
            pl.BlockSpec((n_e, LANES), lambda i: (0, 0)),
        ],
        out_shape=[
            jax.ShapeDtypeStruct((t, d // 2), U32),
            jax.ShapeDtypeStruct((TOP_K, t), I32),
            jax.ShapeDtypeStruct((TOP_K, t), F32),
            jax.ShapeDtypeStruct((TOP_K, t), I32),
            jax.ShapeDtypeStruct((n_e, LANES), F32),
        ],
        scratch_shapes=[pltpu.VMEM((n_e, LANES), F32)],
        compiler_params=_cparams(("arbitrary",)),
        name="router",
    )(x, nw.reshape(1, d), mod, w_router_t, b_router.reshape(n_e, 1), tri)


def _dma_cparams(sem):
    return pltpu.CompilerParams(dimension_semantics=sem, vmem_limit_bytes=V7X_VMEM_LIMIT,
                                disable_bounds_checks=True)


def _dispatch_kernel(dest_ref, hp_ref, buf_ref, xb_ref, sem):
    del buf_ref
    tm = hp_ref.shape[0]

    for t in range(tm):
        for k in range(TOP_K):
            pltpu.make_async_copy(hp_ref.at[pl.ds(t, 1)], xb_ref.at[pl.ds(dest_ref[k, t], 1)], sem).start(
                priority=k % 2)
    for k in range(TOP_K):
        pltpu.make_async_copy(hp_ref, xb_ref.at[pl.ds(0, tm)], sem).wait()


def _dispatch(hp, dest, xb_buf):
    t, half = hp.shape
    tm = _pick(512, t)
    return pl.pallas_call(
        _dispatch_kernel,
        grid=(t // tm,),
        in_specs=[
            pl.BlockSpec((TOP_K, tm), lambda i: (0, i), memory_space=pltpu.SMEM),
            pl.BlockSpec((tm, half), lambda i: (i, 0)),
            pl.BlockSpec(memory_space=pl.ANY),
        ],
        out_specs=pl.BlockSpec(memory_space=pl.ANY),
        out_shape=jax.ShapeDtypeStruct(xb_buf.shape, xb_buf.dtype),
        input_output_aliases={2: 0},
        scratch_shapes=[pltpu.SemaphoreType.DMA],
        compiler_params=_dma_cparams(("arbitrary",)),
        name="dispatch",
    )(dest, hp, xb_buf)


def _expert_kernel(be_ref, nu_ref, nv_ref, xp_ref, wg_ref, bg_ref, wu_ref, bu_ref, wd_ref, bd_ref, o_ref, x_scr):
    i = pl.program_id(0)
    f = pl.program_id(1)
    half = xp_ref.shape[1]

    @pl.when(i >= nu_ref[0])
    def _():
        @pl.when(f == 0)
        def _():
            o_ref[...] = jnp.zeros(o_ref.shape, F32)

    @pl.when(i < nu_ref[0])
    def _():
        @pl.when(f == 0)
        def _():
            keep = lax.broadcasted_iota(I32, (xp_ref.shape[0], 1), 0) < nv_ref[i]
            lo, hi = _unpack_bf16_pairs(jnp.where(keep, xp_ref[...], jnp.uint32(0)))
            x_scr[:, :half] = lo
            x_scr[:, half:] = hi
            o_ref[...] = jnp.broadcast_to(bd_ref[0], o_ref.shape)

        x = x_scr[...]
        gate = jnp.minimum(_dot(x, wg_ref[...]) + bg_ref[0], SWIGLU_LIMIT)
        up = jnp.clip(_dot(x, wu_ref[...]) + bu_ref[0], -SWIGLU_LIMIT, SWIGLU_LIMIT)
        hdn = (up + 1.0) * (gate * jax.nn.sigmoid(SWIGLU_ALPHA * gate))
        o_ref[...] += _dot(hdn.astype(BF16), wd_ref[...].astype(BF16))


def _experts(xb, block_e, n_used, n_valid, layer, w_gate, b_gate, w_up, b_up, w_down, b_down, tm):
    cap, half = xb.shape
    d = 2 * half
    n_e, _, ff = w_gate.shape[1:]
    tf = 512
    nb = cap // tm
    cl = lambda i, nu: jnp.minimum(i, nu[0] - 1)
    wspec_in = pl.BlockSpec((None, None, d, tf), lambda i, f, be, nu, nv: (layer, be[cl(i, nu)], 0, f))
    wspec_out = pl.BlockSpec((None, None, tf, d), lambda i, f, be, nu, nv: (layer, be[cl(i, nu)], f, 0))
    bspec_in = pl.BlockSpec((1, 1, tf), lambda i, f, be, nu, nv: (layer * n_e + be[cl(i, nu)], 0, f))
    bspec_out = pl.BlockSpec((1, 1, d), lambda i, f, be, nu, nv: (layer * n_e + be[cl(i, nu)], 0, 0))
    grid_spec = pltpu.PrefetchScalarGridSpec(
        num_scalar_prefetch=3,
        grid=(nb, ff // tf),
        in_specs=[
            pl.BlockSpec((tm, half), lambda i, f, be, nu, nv: (cl(i, nu), 0)),
            wspec_in, bspec_in, wspec_in, bspec_in, wspec_out, bspec_out,
        ],
        out_specs=pl.BlockSpec((tm, d), lambda i, f, be, nu, nv: (i, 0)),
        scratch_shapes=[pltpu.VMEM((tm, d), BF16)],
    )
    return pl.pallas_call(
        _expert_kernel,
        grid_spec=grid_spec,
        out_shape=jax.ShapeDtypeStruct((cap, d), F32),
        compiler_params=_cparams(("parallel", "arbitrary")),
        name="experts",
    )(block_e, n_used, n_valid, xb, w_gate, b_gate.reshape(-1, 1, ff), w_up, b_up.reshape(-1, 1, ff),
      w_down, b_down.reshape(-1, 1, d))


def _combine_kernel(dcur_ref, dnxt_ref, w_ref, x_ref, mod_ref, yb_ref, o_ref, ybuf, sems, *, kg):
    i = pl.program_id(0)
    n = pl.num_programs(0)
    tm = x_ref.shape[0]
    slot = i % 2

    def fetch(d_ref, s):
        for t in range(tm):
            for k in range(TOP_K):
                pltpu.make_async_copy(yb_ref.at[pl.ds(d_ref[k, t], 1)], ybuf.at[s, k, pl.ds(t, 1)],
                                      sems.at[s]).start(priority=k % 2)

    @pl.when(i == 0)
    def _():
        fetch(dcur_ref, 0)

    for s in range(2):
        @pl.when(jnp.logical_and(i + 1 < n, slot == 1 - s))
        def _():
            fetch(dnxt_ref, s)

    for k in range(TOP_K):
        pltpu.make_async_copy(yb_ref.at[pl.ds(0, tm)], ybuf.at[slot, k], sems.at[slot]).wait()
    w = w_ref[...]
    f = ybuf[slot, 0] * w[:, 0:1]
    for k in range(1, TOP_K):
        f = f + ybuf[slot, k] * w[:, k:k + 1]
    o_ref[...] = x_ref[...] + mod_ref[0][kg:kg + 1, :] * f


def _combine(geo, yb, dest, w_tk, x, mod, kg):
    t, d = x.shape
    tm = _pick(128, geo.t_ctx, geo.seq_lat)
    nsteps = t // tm
    row = geo.mod_row(tm)
    return pl.pallas_call(
        functools.partial(_combine_kernel, kg=kg),
        grid=(nsteps,),
        in_specs=[
            pl.BlockSpec((TOP_K, tm), lambda i: (0, i), memory_space=pltpu.SMEM),
            pl.BlockSpec((TOP_K, tm), lambda i: (0, jnp.minimum(i + 1, nsteps - 1)), memory_space=pltpu.SMEM),
            pl.BlockSpec((tm, TOP_K), lambda i: (i, 0)),
            pl.BlockSpec((tm, d), lambda i: (i, 0)),
            pl.BlockSpec((1, 6, d), lambda i: (row(i), 0, 0)),
            pl.BlockSpec(memory_space=pl.ANY),
        ],
        out_specs=pl.BlockSpec((tm, d), lambda i: (i, 0)),
        out_shape=jax.ShapeDtypeStruct((t, d), F32),
        scratch_shapes=[pltpu.VMEM((2, TOP_K, tm, d), F32), pltpu.SemaphoreType.DMA((2,))],
        compiler_params=_dma_cparams(("arbitrary",)),
        name="combine",
    )(dest, dest, w_tk, x, mod, yb)


MOE_TM = 1024


def _moe_blocks(t, n_e):
    return -(-(t * TOP_K) // MOE_TM) + n_e


def _moe(geo, x, nw, mod, layer, w_router, b_router, w_gate, b_gate, w_up, b_up, w_down, b_down, xb_buf):
    t, d = x.shape
    n_e = w_router.shape[1]
    hp, top_e, top_w, rank, cnt = _router(geo, x, nw, mod, w_router.T, b_router, 3)
    counts = cnt[:, 0].astype(I32)
    padded = (counts + MOE_TM - 1) // MOE_TM * MOE_TM
    pad_end = jnp.cumsum(padded)
    pad_start = pad_end - padded
    nb = _moe_blocks(t, n_e)
    experts = jnp.arange(n_e, dtype=I32)
    start_of = jnp.sum(jnp.where(top_e[None] == experts[:, None, None], pad_start[:, None, None], 0), axis=0)
    dest = start_of + rank
    blk0 = jnp.arange(nb, dtype=I32) * MOE_TM
    block_e = jnp.minimum(jnp.sum((pad_end[None, :] <= blk0[:, None]).astype(I32), axis=1), n_e - 1)
    own = block_e[:, None] == experts[None, :]
    group_end = jnp.sum(jnp.where(own, (pad_start + counts)[None, :], 0), axis=1)
    n_valid = jnp.clip(group_end - blk0, 0, MOE_TM).astype(I32)
    n_used = (pad_end[-1:] // MOE_TM).astype(I32)
    xb = _dispatch(hp, dest, xb_buf)
    yb = _experts(xb, block_e, n_used, n_valid, layer, w_gate, b_gate, w_up, b_up, w_down, b_down, MOE_TM)
    return _combine(geo, yb, dest, top_w.T, x, mod, 5), xb


_SWAP = np.array([(j + ROPE_FREQ) if (j // ROPE_FREQ) % 2 == 0 else (j - ROPE_FREQ) for j in range(QK_ROPE)])


def _rope_table(seq_lat, tq):
    pos = jnp.arange(seq_lat)
    row = (pos // GRID_W).astype(F32)
    colp = (pos % GRID_W).astype(F32)
    inv = jnp.power(ROPE_THETA, -jnp.arange(ROPE_FREQ, dtype=F32) / ROPE_FREQ)
    ar, ac = row[:, None] * inv, colp[:, None] * inv
    cos = jnp.concatenate([jnp.cos(ar), jnp.cos(ar), jnp.cos(ac), jnp.cos(ac)], axis=1)
    sin = jnp.concatenate([-jnp.sin(ar), jnp.sin(ar), -jnp.sin(ac), jnp.sin(ac)], axis=1)
    ident = jnp.concatenate([jnp.ones((tq, QK_ROPE), F32), jnp.zeros((tq, QK_ROPE), F32)], axis=1)
    return jnp.concatenate([ident, jnp.concatenate([cos, sin], axis=1)], axis=0)


def _even_params(w_in, w_qb, q_norm, q_rope_norm, k_rope_norm):
    d = w_in.shape[0]
    o_kv, o_kr, o_conv = Q_LORA, Q_LORA + KV_LORA, Q_LORA + KV_LORA + QK_ROPE
    kr = w_in[:, o_kr:o_conv]
    pad = jnp.zeros((d, EV_N - EV_KR - 2 * QK_ROPE), w_in.dtype)
    w_in_r = jnp.concatenate([w_in[:, o_conv:], w_in[:, :o_kr], kr, kr[:, _SWAP], pad], axis=1).astype(BF16)
    wq = w_qb.reshape(Q_LORA, MLA_HEADS, QK_NOPE + QK_ROPE)
    wq_r = jnp.concatenate([wq, wq[:, :, QK_NOPE:][:, :, _SWAP]], axis=2).reshape(Q_LORA, MLA_HEADS * 256)
    q_nw = jnp.concatenate([q_norm, q_rope_norm, q_rope_norm[_SWAP]]).reshape(1, 256)
    kr_nw = jnp.concatenate([k_rope_norm, k_rope_norm[_SWAP]]).reshape(1, 2 * QK_ROPE)
    return w_in_r, wq_r.astype(BF16), q_nw, kr_nw


def _odd_params(w_in, w_gk, b_gk):
    d = w_in.shape[0]
    pad = jnp.zeros((d, OD_N - w_in.shape[1]), w_in.dtype)
    w_in_r = jnp.concatenate([w_in, pad], axis=1).astype(BF16)
    wpad = jnp.zeros((2, LANES, w_gk.shape[2]), F32)
    wpad = wpad.at[0, :GATE_RANK].set(w_gk[0]).at[1, GATE_RANK:2 * GATE_RANK].set(w_gk[1])
    return w_in_r, wpad.astype(BF16), b_gk.reshape(2, 1, -1)


def kernel(x_prompt, x_sample, cache_mla_ckv, cache_mla_krope, state_gla, c, c_ctx, ada_w, ada_b, norm_mix_w, norm_ffn_w, ev_w_in, ev_q_a_norm, ev_w_qb, ev_kv_a_norm, ev_w_kvb, ev_q_norm, ev_k_norm, ev_q_rope_norm, ev_k_rope_norm, ev_conv_w, ev_w_out, od_w_in, od_w_gk, od_b_gk, od_o_norm, od_w_out, moe_w_router, moe_b_router, moe_w_gate, moe_b_gate, moe_w_up, moe_b_up, moe_w_down, moe_b_down):
    geo = _Geo(x_prompt, x_sample, cache_mla_ckv)
    d = geo.d
    depth = ada_w.shape[0]
    x = jnp.concatenate([x_prompt.reshape(geo.t_ctx, d), x_sample.reshape(geo.t_lat, d)], axis=0)
    cond = jnp.concatenate([c_ctx[None, :], c, jnp.zeros((16 - 1 - geo.n_lat, d), F32)], axis=0)
    mods = _adaln(cond, ada_w, ada_b).reshape(depth, 16, 6, d)
    tq = _pick(512, geo.t_ctx, geo.seq_lat)
    tab = _rope_table(geo.seq_lat, tq)
    w_gate, w_up, w_down = moe_w_gate.astype(BF16), moe_w_up.astype(BF16), moe_w_down
    xb_buf = jnp.zeros((_moe_blocks(geo.t, moe_w_router.shape[2]) * MOE_TM, d // 2), U32)
    attn_buf = jnp.zeros((geo.t, MLA_HEADS * V_HEAD), BF16)
    gla_bufs = (jnp.zeros((geo.t, GLA_HEADS * GLA_DVH), BF16), jnp.zeros((geo.t, GLA_HEADS * GLA_DVH), BF16))
    new_ckv, new_krope, new_gla = [], [], []
    for l in range(depth):
        i = l // 2
        mod = mods[l]
        if l % 2 == 0:
            w_in_r, wq_r, q_nw, kr_nw = _even_params(ev_w_in[i], ev_w_qb[i], ev_q_norm[i], ev_q_rope_norm[i],
                                                     ev_k_rope_norm[i])
            z = _lnmod_mm(geo, x, norm_mix_w[l], mod, w_in_r, 0)
            conv, qan, ckv, krn, krot = _even_mid(geo, z, ev_conv_w[i], ev_q_a_norm[i], ev_kv_a_norm[i], kr_nw,
                                                  tab, tq)
            q256 = _q_proj(geo, qan, wq_r, q_nw, tab, tq)
            kc = cache_mla_krope[:, i].reshape(geo.n_lat * geo.past, QK_ROPE)
            ckv_all = jnp.concatenate([ckv, cache_mla_ckv[:, i].reshape(geo.n_lat * geo.past, KV_LORA)], axis=0)
            krot_all = jnp.concatenate([krot, jnp.concatenate([kc, kc], axis=1).astype(BF16)], axis=0)
            k256, v256 = _kv_proj(ckv_all, ev_w_kvb[i].astype(BF16), ev_k_norm[i], krot_all)
            attn_buf = _attention(geo, q256, k256, v256, attn_buf)
            x = _mix_out(geo, attn_buf, conv, ev_w_out[i].astype(BF16), x, mod, 2)
            new_ckv.append(ckv[:geo.t_ctx].reshape(geo.n_ctx, geo.seq_ctx, KV_LORA))
            new_krope.append(krn[:geo.t_ctx].reshape(geo.n_ctx, geo.seq_ctx, QK_ROPE))
        else:
            w_in_r, wgk_pad, bgk = _odd_params(od_w_in[i], od_w_gk[i], od_b_gk[i])
            z = _lnmod_mm(geo, x, norm_mix_w[l], mod, w_in_r, 0)
            o_f, o_b, s_fin = _gla_scan(z, wgk_pad, bgk, geo.n_ctx, geo.seq_ctx, 0, None, True, gla_bufs)
            gla_bufs = _gla_scan(z, wgk_pad, bgk, geo.n_lat, geo.seq_lat, geo.t_ctx, state_gla[:, i], False,
                                 (o_f, o_b))
            x = _gla_out(geo, gla_bufs[0], gla_bufs[1], z, od_o_norm[i], od_w_out[i].astype(BF16), x, mod, 2)
            new_gla.append(s_fin)
        x, xb_buf = _moe(geo, x, norm_ffn_w[l], mod, l, moe_w_router[l], moe_b_router[l], w_gate, moe_b_gate,
                         w_up, moe_b_up, w_down, moe_b_down, xb_buf)
    xp = x[:geo.t_ctx].reshape(x_prompt.shape)
    xs = x[geo.t_ctx:].reshape(x_sample.shape)
    return (xp, xs, jnp.stack(new_ckv, axis=1), jnp.stack(new_krope, axis=1), jnp.stack(new_gla, axis=1))
```

```python
import functools

import jax
import jax.numpy as jnp
import numpy as np
from jax import lax
from jax.experimental import pallas as pl
from jax.experimental.pallas import tpu as pltpu

F32 = jnp.float32
BF16 = jnp.bfloat16
I32 = jnp.int32
U32 = jnp.uint32

EPS = 1e-6
GRID_W = 64
MLA_HEADS = 8
QK_NOPE = 128
QK_ROPE = 64
V_HEAD = 128
Q_LORA = 512
KV_LORA = 256
ROPE_THETA = 10000.0
ROPE_FREQ = QK_ROPE // 4
CONV_W = 1024
GLA_HEADS = 4
GLA_DKH = 256
GLA_DVH = 512
GATE_RANK = 16
GATE_NORM = 16.0
GLA_CHUNK = 64
TOP_K = 4
SWIGLU_ALPHA = 1.702
SWIGLU_LIMIT = 7.0

V7X_VMEM_LIMIT = 56 * 1024 * 1024
LANES = 128
HI = lax.Precision.HIGHEST


def _cparams(sem):
    return pltpu.CompilerParams(dimension_semantics=sem, vmem_limit_bytes=V7X_VMEM_LIMIT)


def _pick(want, *dims):
    t = want
    while any(d % t for d in dims):
        t //= 2
    return t


def _dot(a, b):
    return jnp.dot(a, b, preferred_element_type=F32)


def _rms(x, w):
    return x * lax.rsqrt(jnp.mean(x * x, axis=-1, keepdims=True) + EPS) * w


def _lnmod(x, nw, mod, ks):
    gain = nw * (1.0 + mod[ks + 1:ks + 2, :])
    return x * lax.rsqrt(jnp.mean(x * x, axis=-1, keepdims=True) + EPS) * gain + mod[ks:ks + 1, :]


class _Geo:
    def __init__(self, x_prompt, x_sample, cache_ckv):
        self.n_ctx, self.seq_ctx, self.d = x_prompt.shape
        self.n_lat, self.seq_lat, _ = x_sample.shape
        self.past = cache_ckv.shape[2]
        self.t_ctx = self.n_ctx * self.seq_ctx
        self.t_lat = self.n_lat * self.seq_lat
        self.t = self.t_ctx + self.t_lat

    def mod_row(self, tm):
        t_ctx, seq_lat = self.t_ctx, self.seq_lat
        return lambda i: jnp.where(i * tm < t_ctx, 0, 1 + (i * tm - t_ctx) // seq_lat)


def _adaln_kernel(c_ref, w_ref, b_ref, o_ref):
    c = c_ref[...]
    a = (c * jax.nn.sigmoid(c)).astype(BF16)
    o_ref[0] = _dot(a, w_ref[0].astype(BF16)) + b_ref[0]


def _adaln(cond16, ada_w, ada_b):
    depth, d, n = ada_w.shape
    tn = 1024
    return pl.pallas_call(
        _adaln_kernel,
        grid=(depth, n // tn),
        in_specs=[
            pl.BlockSpec((16, d), lambda l, j: (0, 0)),
            pl.BlockSpec((1, d, tn), lambda l, j: (l, 0, j)),
            pl.BlockSpec((1, 1, tn), lambda l, j: (l, 0, j)),
        ],
        out_specs=pl.BlockSpec((1, 16, tn), lambda l, j: (l, 0, j)),
        out_shape=jax.ShapeDtypeStruct((depth, 16, n), F32),
        compiler_params=_cparams(("parallel", "parallel")),
        name="adaln",
    )(cond16, ada_w, ada_b.reshape(depth, 1, n))


def _lnmod_mm_kernel(x_ref, nw_ref, mod_ref, w_ref, o_ref, h_scr, *, ks):
    @pl.when(pl.program_id(1) == 0)
    def _():
        h_scr[...] = _lnmod(x_ref[...], nw_ref[...], mod_ref[0], ks).astype(BF16)

    o_ref[...] = _dot(h_scr[...], w_ref[...])


def _lnmod_mm(geo, x, nw, mod, w, ks):
    t, d = x.shape
    n = w.shape[1]
    tm = _pick(1024, geo.t_ctx, geo.seq_lat)
    tn = max(c for c in range(LANES, 1024 + 1, LANES) if n % c == 0)
    row = geo.mod_row(tm)
    return pl.pallas_call(
        functools.partial(_lnmod_mm_kernel, ks=ks),
        grid=(t // tm, n // tn),
        in_specs=[
            pl.BlockSpec((tm, d), lambda i, j: (i, 0)),
            pl.BlockSpec((1, d), lambda i, j: (0, 0)),
            pl.BlockSpec((1, 6, d), lambda i, j: (row(i), 0, 0)),
            pl.BlockSpec((d, tn), lambda i, j: (0, j)),
        ],
        out_specs=pl.BlockSpec((tm, tn), lambda i, j: (i, j)),
        out_shape=jax.ShapeDtypeStruct((t, n), F32),
        scratch_shapes=[pltpu.VMEM((tm, d), BF16)],
        compiler_params=_cparams(("parallel", "arbitrary")),
        name="lnmod_mm",
    )(x, nw.reshape(1, d), mod, w)


EV_QA = 3 * CONV_W
EV_KVA = EV_QA + Q_LORA
EV_KR = EV_KVA + KV_LORA
EV_N = 4096


def _even_mid_kernel(bg_ref, cg_ref, xv_ref, cgp_ref, xvp_ref, cgn_ref, xvn_ref, qa_ref, kva_ref, kr_ref,
                     cw_ref, qaw_ref, kvw_ref, krw_ref, tab_ref,
                     conv_ref, qan_ref, ckv_ref, krn_ref, krot_ref, *, t_ctx, seq_ctx, seq_lat):
    i = pl.program_id(0)
    tm = bg_ref.shape[0]
    u = cg_ref[...] * xv_ref[...]
    rows = lax.broadcasted_iota(I32, (tm, 1), 0)
    tok = i * tm + rows
    pos = jnp.where(tok < t_ctx, tok & (seq_ctx - 1), (tok - t_ctx) & (seq_lat - 1))
    last = jnp.where(tok < t_ctx, seq_ctx - 1, seq_lat - 1)
    u_m1 = jnp.where(rows == 0, cgp_ref[7:8, :] * xvp_ref[7:8, :], pltpu.roll(u, 1, axis=0))
    u_p1 = jnp.where(rows == tm - 1, cgn_ref[0:1, :] * xvn_ref[0:1, :], pltpu.roll(u, tm - 1, axis=0))
    u_m1 = jnp.where(pos == 0, 0.0, u_m1)
    u_p1 = jnp.where(pos == last, 0.0, u_p1)
    cw = cw_ref[...]
    conv = bg_ref[...] * (u_m1 * cw[0:1, :] + u * cw[1:2, :] + u_p1 * cw[2:3, :])
    conv_ref[...] = conv.astype(BF16)
    qan_ref[...] = _rms(qa_ref[...], qaw_ref[...]).astype(BF16)
    ckv_ref[...] = _rms(kva_ref[...], kvw_ref[...])
    kr = kr_ref[...]
    krn = _rms(kr, krw_ref[...])
    krn_ref[...] = krn[:, :QK_ROPE]
    y = krn * tab_ref[...]
    krot_ref[...] = (y + pltpu.roll(y, QK_ROPE, axis=1)).astype(BF16)


def _even_mid(geo, z, conv_w, qa_w, kv_w, kr_w128, tab, tq):
    t = z.shape[0]
    tm = tq
    ncb = geo.t_ctx // tm
    bps = geo.seq_lat // tm
    nb8 = t // 8
    r8 = tm // 8
    assert geo.seq_ctx & (geo.seq_ctx - 1) == 0 and geo.seq_lat & (geo.seq_lat - 1) == 0
    tab_idx = lambda i: jnp.where(i < ncb, 0, 1 + (i - ncb) % bps)
    col = lambda c: (lambda i: (i, c))
    in_specs = [
        pl.BlockSpec((tm, CONV_W), col(0)),
        pl.BlockSpec((tm, CONV_W), col(1)),
        pl.BlockSpec((tm, CONV_W), col(2)),
        pl.BlockSpec((8, CONV_W), lambda i: (jnp.maximum(i * r8 - 1, 0), 1)),
        pl.BlockSpec((8, CONV_W), lambda i: (jnp.maximum(i * r8 - 1, 0), 2)),
        pl.BlockSpec((8, CONV_W), lambda i: (jnp.minimum((i + 1) * r8, nb8 - 1), 1)),
        pl.BlockSpec((8, CONV_W), lambda i: (jnp.minimum((i + 1) * r8, nb8 - 1), 2)),
        pl.BlockSpec((tm, Q_LORA), col(EV_QA // Q_LORA)),
        pl.BlockSpec((tm, KV_LORA), col(EV_KVA // KV_LORA)),
        pl.BlockSpec((tm, 2 * QK_ROPE), col(EV_KR // (2 * QK_ROPE))),
        pl.BlockSpec((3, CONV_W), lambda i: (0, 0)),
        pl.BlockSpec((1, Q_LORA), lambda i: (0, 0)),
        pl.BlockSpec((1, KV_LORA), lambda i: (0, 0)),
        pl.BlockSpec((1, 2 * QK_ROPE), lambda i: (0, 0)),
        pl.BlockSpec((tm, 2 * QK_ROPE), lambda i: (tab_idx(i), 0)),
    ]
    out_specs = [
        pl.BlockSpec((tm, CONV_W), lambda i: (i, 0)),
        pl.BlockSpec((tm, Q_LORA), lambda i: (i, 0)),
        pl.BlockSpec((tm, KV_LORA), lambda i: (i, 0)),
        pl.BlockSpec((tm, QK_ROPE), lambda i: (i, 0)),
        pl.BlockSpec((tm, 2 * QK_ROPE), lambda i: (i, 0)),
    ]
    out_shape = [
        jax.ShapeDtypeStruct((t, CONV_W), BF16),
        jax.ShapeDtypeStruct((t, Q_LORA), BF16),
        jax.ShapeDtypeStruct((t, KV_LORA), F32),
        jax.ShapeDtypeStruct((t, QK_ROPE), F32),
        jax.ShapeDtypeStruct((t, 2 * QK_ROPE), BF16),
    ]
    return pl.pallas_call(
        functools.partial(_even_mid_kernel, t_ctx=geo.t_ctx, seq_ctx=geo.seq_ctx, seq_lat=geo.seq_lat),
        grid=(t // tm,),
        in_specs=in_specs,
        out_specs=out_specs,
        out_shape=out_shape,
        compiler_params=_cparams(("parallel",)),
        name="even_mid",
    )(z, z, z, z, z, z, z, z, z, z, conv_w, qa_w.reshape(1, -1), kv_w.reshape(1, -1), kr_w128, tab)


def _q_proj_kernel(a_ref, w_ref, nw_ref, tab_ref, o_ref):
    acc = _dot(a_ref[...], w_ref[...])
    nw = nw_ref[...]
    tab = tab_ref[...]
    for h in range(MLA_HEADS):
        n = acc[:, h * 256:h * 256 + QK_NOPE]
        r = acc[:, h * 256 + QK_NOPE:(h + 1) * 256]
        qn = _rms(n, nw[:, :QK_NOPE])
        y = _rms(r, nw[:, QK_NOPE:]) * tab
        o_ref[h] = (jnp.concatenate([qn, y], axis=-1) * Q_PRESCALE).astype(BF16)


def _q_proj(geo, qan, w_qb, nw256, tab, tq):
    t = qan.shape[0]
    tm = tq
    ncb = geo.t_ctx // tm
    bps = geo.seq_lat // tm
    n = w_qb.shape[1]
    return pl.pallas_call(
        _q_proj_kernel,
        grid=(t // tm,),
        in_specs=[
            pl.BlockSpec((tm, Q_LORA), lambda i: (i, 0)),
            pl.BlockSpec((Q_LORA, n), lambda i: (0, 0)),
            pl.BlockSpec((1, 256), lambda i: (0, 0)),
            pl.BlockSpec((tm, 2 * QK_ROPE), lambda i: (jnp.where(i < ncb, 0, 1 + (i - ncb) % bps), 0)),
        ],
        out_specs=pl.BlockSpec((MLA_HEADS, tm, 256), lambda i: (0, i, 0)),
        out_shape=jax.ShapeDtypeStruct((MLA_HEADS, t, 256), BF16),
        compiler_params=_cparams(("parallel",)),
        name="q_proj",
    )(qan, w_qb, nw256, tab)


def _kv_proj_kernel(a_ref, w_ref, nw_ref, kr_ref, k_ref, v_ref):
    acc = _dot(a_ref[...].astype(BF16), w_ref[...])
    kr = kr_ref[...]
    ones = jnp.ones((acc.shape[0], V_HEAD), BF16)
    for h in range(MLA_HEADS):
        kn = _rms(acc[:, h * 256:h * 256 + QK_NOPE], nw_ref[...]).astype(BF16)
        k_ref[h] = jnp.concatenate([kn, kr], axis=-1)
        v_ref[h] = jnp.concatenate([acc[:, h * 256 + QK_NOPE:(h + 1) * 256].astype(BF16), ones], axis=-1)


def _kv_proj(ckv_all, w_kvb, k_norm, krot_all):
    t = ckv_all.shape[0]
    tm = _pick(512, t)
    n = w_kvb.shape[1]
    return pl.pallas_call(
        _kv_proj_kernel,
        grid=(t // tm,),
        in_specs=[
            pl.BlockSpec((tm, KV_LORA), lambda i: (i, 0)),
            pl.BlockSpec((KV_LORA, n), lambda i: (0, 0)),
            pl.BlockSpec((1, QK_NOPE), lambda i: (0, 0)),
            pl.BlockSpec((tm, 2 * QK_ROPE), lambda i: (i, 0)),
        ],
        out_specs=[
            pl.BlockSpec((MLA_HEADS, tm, 256), lambda i: (0, i, 0)),
            pl.BlockSpec((MLA_HEADS, tm, 2 * V_HEAD), lambda i: (0, i, 0)),
        ],
        out_shape=[
            jax.ShapeDtypeStruct((MLA_HEADS, t, 256), BF16),
            jax.ShapeDtypeStruct((MLA_HEADS, t, 2 * V_HEAD), BF16),
        ],
        compiler_params=_cparams(("parallel",)),
        name="kv_proj",
    )(ckv_all, w_kvb, k_norm.reshape(1, -1), krot_all)


_NT = (((1,), (1,)), ((), ()))
_TN = (((0,), (0,)), ((), ()))
Q_PRESCALE = (QK_NOPE + QK_ROPE) ** -0.5 * 1.4426950408889634
ATTN_KEY_CHUNK = 256


def _softmax_step(q, k, v, m, acc):
    s = lax.dot_general(q, k, _NT, preferred_element_type=F32)
    m_new = jnp.maximum(m, jnp.max(s, axis=-1, keepdims=True))
    p = jnp.exp2(s - m_new)
    acc = acc * jnp.exp2(m - m_new) + _dot(p.astype(BF16), v)
    return m_new, acc


def _attn_ctx_kernel(q_ref, k_ref, v_ref, prev_ref, o_ref):
    del prev_ref
    outs = []
    for h in range(MLA_HEADS):
        s = lax.dot_general(q_ref[h], k_ref[h], _NT, preferred_element_type=F32)
        p = jnp.exp2(s - jnp.max(s, axis=-1, keepdims=True))
        o = _dot(p.astype(BF16), v_ref[h])
        outs.append(o[:, :V_HEAD] / o[:, V_HEAD:V_HEAD + 1])
    o_ref[...] = jnp.concatenate(outs, axis=-1).astype(BF16)


def _attn_lat_kernel(q_ref, kc_ref, vc_ref, kl_ref, vl_ref, prev_ref, o_ref):
    del prev_ref
    q = q_ref[0]
    tq = q.shape[0]
    m = jnp.full((tq, 1), -jnp.inf, F32)
    acc = jnp.zeros((tq, 2 * V_HEAD), F32)
    for k_ref, v_ref in ((kc_ref, vc_ref), (kl_ref, vl_ref)):
        n = k_ref.shape[1]
        ck = min(ATTN_KEY_CHUNK, n)
        for j in range(n // ck):
            m, acc = _softmax_step(q, k_ref[0, j * ck:(j + 1) * ck, :], v_ref[0, j * ck:(j + 1) * ck, :], m, acc)
    o_ref[...] = (acc[:, :V_HEAD] / acc[:, V_HEAD:V_HEAD + 1]).astype(BF16)


def _attention(geo, q256, k256, v256, o_buf):
    t = geo.t
    sc, sl, past = geo.seq_ctx, geo.seq_lat, geo.past
    hv = MLA_HEADS * V_HEAD
    vw = 2 * V_HEAD
    o_ctx = pl.pallas_call(
        _attn_ctx_kernel,
        grid=(geo.n_ctx,),
        in_specs=[
            pl.BlockSpec((MLA_HEADS, sc, 256), lambda b: (0, b, 0)),
            pl.BlockSpec((MLA_HEADS, sc, 256), lambda b: (0, b, 0)),
            pl.BlockSpec((MLA_HEADS, sc, vw), lambda b: (0, b, 0)),
            pl.BlockSpec(memory_space=pl.ANY),
        ],
        out_specs=pl.BlockSpec((sc, hv), lambda b: (b, 0)),
        out_shape=jax.ShapeDtypeStruct((t, hv), BF16),
        input_output_aliases={3: 0},
        compiler_params=_cparams(("parallel",)),
        name="attn_ctx",
    )(q256, k256, v256, o_buf)
    tq = _pick(1024, sl, geo.t_ctx)
    nq = sl // tq
    q_off = geo.t_ctx // tq
    kl_off = geo.t_ctx // sl
    kc_off = t // past
    return pl.pallas_call(
        _attn_lat_kernel,
        grid=(geo.n_lat, MLA_HEADS, nq),
        in_specs=[
            pl.BlockSpec((1, tq, 256), lambda b, h, i: (h, q_off + b * nq + i, 0)),
            pl.BlockSpec((1, past, 256), lambda b, h, i: (h, kc_off + b, 0)),
            pl.BlockSpec((1, past, vw), lambda b, h, i: (h, kc_off + b, 0)),
            pl.BlockSpec((1, sl, 256), lambda b, h, i: (h, kl_off + b, 0)),
            pl.BlockSpec((1, sl, vw), lambda b, h, i: (h, kl_off + b, 0)),
            pl.BlockSpec(memory_space=pl.ANY),
        ],
        out_specs=pl.BlockSpec((tq, V_HEAD), lambda b, h, i: (q_off + b * nq + i, h)),
        out_shape=jax.ShapeDtypeStruct((t, hv), BF16),
        input_output_aliases={5: 0},
        compiler_params=_cparams(("parallel", "parallel", "arbitrary")),
        name="attn_lat",
    )(q256, k256, v256, k256, v256, o_ctx)


def _mix_out_kernel(a1_ref, a2_ref, w_ref, x_ref, mod_ref, o_ref, *, kg):
    k1 = a1_ref.shape[1]
    acc = _dot(a1_ref[...], w_ref[:k1, :]) + _dot(a2_ref[...], w_ref[k1:, :])
    o_ref[...] = x_ref[...] + mod_ref[0][kg:kg + 1, :] * acc


def _mix_out(geo, a1, a2, w, x, mod, kg):
    t, d = x.shape
    tm = _pick(512, geo.t_ctx, geo.seq_lat)
    row = geo.mod_row(tm)
    k1, k2 = a1.shape[1], a2.shape[1]
    return pl.pallas_call(
        functools.partial(_mix_out_kernel, kg=kg),
        grid=(t // tm,),
        in_specs=[
            pl.BlockSpec((tm, k1), lambda i: (i, 0)),
            pl.BlockSpec((tm, k2), lambda i: (i, 0)),
            pl.BlockSpec((k1 + k2, d), lambda i: (0, 0)),
            pl.BlockSpec((tm, d), lambda i: (i, 0)),
            pl.BlockSpec((1, 6, d), lambda i: (row(i), 0, 0)),
        ],
        out_specs=pl.BlockSpec((tm, d), lambda i: (i, 0)),
        out_shape=jax.ShapeDtypeStruct((t, d), F32),
        compiler_params=_cparams(("parallel",)),
        name="mix_out",
    )(a1, a2, w, x, mod)


OD_K = GLA_HEADS * GLA_DKH
OD_V = 2 * OD_K
OD_G = OD_V + GLA_HEADS * GLA_DVH
OD_R = OD_G + GLA_HEADS * GLA_DVH
OD_N = OD_R + LANES


def _log_sigmoid(x):
    return jnp.minimum(x, 0.0) - jnp.log1p(jnp.exp(-jnp.abs(x)))


def _split3(x):
    hi = x.astype(BF16)
    r1 = x - hi.astype(F32)
    mid = r1.astype(BF16)
    lo = (r1 - mid.astype(F32)).astype(BF16)
    return hi, mid, lo


def _gla_chunk(q, k, v, r, wgk, bgk, st_ref, o_ref, backward):
    c = q.shape[0]
    la = _log_sigmoid(_dot(r.astype(BF16), wgk) + bgk) * (1.0 / GATE_NORM)
    ii = lax.broadcasted_iota(I32, (c, c), 0)
    jj = lax.broadcasted_iota(I32, (c, c), 1)
    causal = (ii <= jj) if backward else (ii >= jj)
    tri = causal.astype(BF16)
    hi, mid, lo = _split3(la)
    b = _dot(tri, hi) + _dot(tri, mid) + _dot(tri, lo)
    tot = jnp.sum(la, axis=0, keepdims=True)
    q_dec = (q * (GLA_DKH ** -0.5) * jnp.exp(b)).astype(BF16)
    k_dec = (k * jnp.exp(-b)).astype(BF16)
    k_end = (k * jnp.exp(tot - b)).astype(BF16)
    e_tot = jnp.exp(tot)
    vb = v.astype(BF16)
    heads = range(GLA_HEADS)
    ks = [slice(h * GLA_DKH, (h + 1) * GLA_DKH) for h in heads]
    vs = [slice(h * GLA_DVH, (h + 1) * GLA_DVH) for h in heads]
    scores = [jnp.where(causal, lax.dot_general(q_dec[:, ks[h]], k_dec[:, ks[h]], _NT, preferred_element_type=F32),
                        0.0).astype(BF16) for h in heads]
    st = [st_ref[h] for h in heads]
    for h in heads:
        o_ref[:, vs[h]] = (_dot(scores[h], vb[:, vs[h]]) + lax.dot_general(
            q_dec[:, ks[h]], st[h].astype(BF16), _NT, preferred_element_type=F32)).astype(o_ref.dtype)
    for h in heads:
        st_ref[h] = e_tot[:, ks[h]] * st[h] + lax.dot_general(vb[:, vs[h]], k_end[:, ks[h]], _TN,
                                                             preferred_element_type=F32)


def _gla_kernel(*refs, has_init, want_final):
    qf, kf, vf, rf, qb, kb, vb, rb, wgk_ref, bgk_ref = refs[:10]
    pos = 10
    s0_ref = None
    if has_init:
        s0_ref = refs[pos]
        pos += 1
    pos += 2
    of_ref, ob_ref = refs[pos], refs[pos + 1]
    pos += 2
    sfin_ref = None
    if want_final:
        sfin_ref = refs[pos]
        pos += 1
    st_scr = refs[pos]
    c = pl.program_id(1)
    nc = pl.num_programs(1)

    @pl.when(c == 0)
    def _():
        for d in range(2):
            for h in range(GLA_HEADS):
                if has_init:
                    st_scr[d, h] = s0_ref[0, d, h].T
                else:
                    st_scr[d, h] = jnp.zeros(st_scr.shape[2:], F32)

    _gla_chunk(qf[...], kf[...], vf[...], rf[...], wgk_ref[0], bgk_ref[0], st_scr.at[0], of_ref, False)
    _gla_chunk(qb[...], kb[...], vb[...], rb[...], wgk_ref[1], bgk_ref[1], st_scr.at[1], ob_ref, True)

    if want_final:
        @pl.when(c == nc - 1)
        def _():
            for d in range(2):
                for h in range(GLA_HEADS):
                    sfin_ref[0, d, h] = st_scr[d, h].T


def _gla_scan(z, wgk_pad, bgk, n_seq, seq, row0, s0, want_final, bufs):
    t = z.shape[0]
    c = GLA_CHUNK
    nc = seq // c
    base = row0 // c
    fwd = lambda b, i: base + b * nc + i
    bwd = lambda b, i: base + b * nc + (nc - 1 - i)
    dk, dv = GLA_HEADS * GLA_DKH, GLA_HEADS * GLA_DVH

    def zspecs(rowf):
        return [
            pl.BlockSpec((c, dk), lambda b, i: (rowf(b, i), 0)),
            pl.BlockSpec((c, dk), lambda b, i: (rowf(b, i), OD_K // dk)),
            pl.BlockSpec((c, dv), lambda b, i: (rowf(b, i), OD_V // dv)),
            pl.BlockSpec((c, LANES), lambda b, i: (rowf(b, i), OD_R // LANES)),
        ]

    in_specs = zspecs(fwd) + zspecs(bwd) + [
        pl.BlockSpec((2, LANES, dk), lambda b, i: (0, 0, 0)),
        pl.BlockSpec((2, 1, dk), lambda b, i: (0, 0, 0)),
    ]
    args = [z] * 8 + [wgk_pad, bgk]
    if s0 is not None:
        in_specs.append(pl.BlockSpec((1, 2, GLA_HEADS, GLA_DKH, GLA_DVH), lambda b, i: (b, 0, 0, 0, 0)))
        args.append(s0)
    aliases = {len(args): 0, len(args) + 1: 1}
    in_specs += [pl.BlockSpec(memory_space=pl.ANY)] * 2
    args += list(bufs)
    out_specs = [pl.BlockSpec((c, dv), lambda b, i: (fwd(b, i), 0)), pl.BlockSpec((c, dv), lambda b, i: (bwd(b, i), 0))]
    out_shape = [jax.ShapeDtypeStruct((t, dv), bufs[0].dtype)] * 2
    if want_final:
        out_specs.append(pl.BlockSpec((1, 2, GLA_HEADS, GLA_DKH, GLA_DVH), lambda b, i: (b, 0, 0, 0, 0)))
        out_shape.append(jax.ShapeDtypeStruct((n_seq, 2, GLA_HEADS, GLA_DKH, GLA_DVH), F32))
    return pl.pallas_call(
        functools.partial(_gla_kernel, has_init=s0 is not None, want_final=want_final),
        grid=(n_seq, nc),
        in_specs=in_specs,
        out_specs=out_specs,
        out_shape=out_shape,
        input_output_aliases=aliases,
        scratch_shapes=[pltpu.VMEM((2, GLA_HEADS, GLA_DVH, GLA_DKH), F32)],
        compiler_params=_cparams(("parallel", "arbitrary")),
        name="gla_scan",
    )(*args)


def _gla_out_kernel(of_ref, ob_ref, g_ref, nw_ref, w_ref, x_ref, mod_ref, o_ref, a_scr, *, kg):
    @pl.when(pl.program_id(1) == 0)
    def _():
        nw = nw_ref[...]
        for h in range(GLA_HEADS):
            sl = slice(h * GLA_DVH, (h + 1) * GLA_DVH)
            o = _rms(of_ref[:, sl].astype(F32) + ob_ref[:, sl].astype(F32), nw)
            g = g_ref[:, sl]
            a_scr[:, sl] = (o * (g * jax.nn.sigmoid(g))).astype(BF16)

    o_ref[...] = x_ref[...] + mod_ref[0][kg:kg + 1, :] * _dot(a_scr[...], w_ref[...])


def _gla_out(geo, o_f, o_b, z, o_norm, w_out, x, mod, kg):
    t, d = x.shape
    dv = GLA_HEADS * GLA_DVH
    tm = _pick(512, geo.t_ctx, geo.seq_lat)
    tn = d
    row = geo.mod_row(tm)
    return pl.pallas_call(
        functools.partial(_gla_out_kernel, kg=kg),
        grid=(t // tm, d // tn),
        in_specs=[
            pl.BlockSpec((tm, dv), lambda i, j: (i, 0)),
            pl.BlockSpec((tm, dv), lambda i, j: (i, 0)),
            pl.BlockSpec((tm, dv), lambda i, j: (i, OD_G // dv)),
            pl.BlockSpec((1, GLA_DVH), lambda i, j: (0, 0)),
            pl.BlockSpec((dv, tn), lambda i, j: (0, j)),
            pl.BlockSpec((tm, tn), lambda i, j: (i, j)),
            pl.BlockSpec((1, 6, tn), lambda i, j: (row(i), 0, j)),
        ],
        out_specs=pl.BlockSpec((tm, tn), lambda i, j: (i, j)),
        out_shape=jax.ShapeDtypeStruct((t, d), F32),
        scratch_shapes=[pltpu.VMEM((tm, dv), BF16)],
        compiler_params=_cparams(("parallel", "arbitrary")),
        name="gla_out",
    )(o_f, o_b, z, o_norm.reshape(1, -1), w_out, x, mod)


def _pack_bf16_pairs(h):
    n = h.shape[1] // 2
    lo = pltpu.bitcast(h[:, :n].astype(BF16).astype(F32), U32)
    hi = pltpu.bitcast(h[:, n:].astype(BF16).astype(F32), U32)
    return (lo >> 16) | (hi & jnp.uint32(0xFFFF0000))


def _unpack_bf16_pairs(w):
    lo = pltpu.bitcast(w << 16, F32).astype(BF16)
    hi = pltpu.bitcast(w & jnp.uint32(0xFFFF0000), F32).astype(BF16)
    return lo, hi


def _router_kernel(x_ref, nw_ref, mod_ref, wr_ref, br_ref, tri_ref,
                   hp_ref, e_ref, w_ref, rank_ref, cnt_ref, carry_scr, *, ks):
    @pl.when(pl.program_id(0) == 0)
    def _():
        carry_scr[...] = jnp.zeros(carry_scr.shape, F32)

    h = _lnmod(x_ref[...], nw_ref[...], mod_ref[0], ks)
    hp_ref[...] = _pack_bf16_pairs(h)
    lg = lax.dot_general(wr_ref[...], h, _NT, precision=HI, preferred_element_type=F32) + br_ref[...]
    n_e, tm = lg.shape
    rows = lax.broadcasted_iota(I32, (n_e, tm), 0).astype(F32)
    vals, sels, hits = [], [], []
    for _ in range(TOP_K):
        m = jnp.max(lg, axis=0, keepdims=True)
        idx = jnp.min(jnp.where(lg == m, rows, float(n_e)), axis=0, keepdims=True)
        hit = rows == idx
        vals.append(m)
        sels.append(idx)
        hits.append(hit)
        lg = jnp.where(hit, -jnp.inf, lg)
    ex = [jnp.exp(v - vals[0]) for v in vals]
    den = ex[0] + ex[1] + ex[2] + ex[3]
    w_ref[...] = jnp.concatenate([e / den for e in ex], axis=0)
    e_ref[...] = jnp.concatenate(sels, axis=0).astype(I32)
    onehot = jnp.zeros((n_e, tm), F32)
    for hit in hits:
        onehot = jnp.where(hit, 1.0, onehot)
    before = carry_scr[:, 0:1] + _dot(onehot.astype(BF16), tri_ref[...])
    rank_ref[...] = jnp.concatenate(
        [jnp.sum(jnp.where(hit, before, 0.0), axis=0, keepdims=True) for hit in hits], axis=0).astype(I32)
    carry_scr[...] = carry_scr[...] + jnp.sum(onehot, axis=1, keepdims=True)
    cnt_ref[...] = carry_scr[...]


def _router(geo, x, nw, mod, w_router_t, b_router, ks):
    t, d = x.shape
    n_e = w_router_t.shape[0]
    tm = _pick(512, geo.t_ctx, geo.seq_lat)
    row = geo.mod_row(tm)
    tri = (lax.broadcasted_iota(I32, (tm, tm), 0) < lax.broadcasted_iota(I32, (tm, tm), 1)).astype(BF16)
    return pl.pallas_call(
        functools.partial(_router_kernel, ks=ks),
        grid=(t // tm,),
        in_specs=[
            pl.BlockSpec((tm, d), lambda i: (i, 0)),
            pl.BlockSpec((1, d), lambda i: (0, 0)),
            pl.BlockSpec((1, 6, d), lambda i: (row(i), 0, 0)),
            pl.BlockSpec((n_e, d), lambda i: (0, 0)),
            pl.BlockSpec((n_e, 1), lambda i: (0, 0)),
            pl.BlockSpec((tm, tm), lambda i: (0, 0)),
        ],
        out_specs=[
            pl.BlockSpec((tm, d // 2), lambda i: (i, 0)),
            pl.BlockSpec((TOP_K, tm), lambda i: (0, i)),
            pl.BlockSpec((TOP_K, tm), lambda i: (0, i)),
            pl.BlockSpec((TOP_K, tm), lambda i: (0, i)),
            pl.BlockSpec((n_e, LANES), lambda i: (0, 0)),
        ],
        out_shape=[
            jax.ShapeDtypeStruct((t, d // 2), U32),
            jax.ShapeDtypeStruct((TOP_K, t), I32),
            jax.ShapeDtypeStruct((TOP_K, t), F32),
            jax.ShapeDtypeStruct((TOP_K, t), I32),
            jax.ShapeDtypeStruct((n_e, LANES), F32),
        ],
        scratch_shapes=[pltpu.VMEM((n_e, LANES), F32)],
        compiler_params=_cparams(("arbitrary",)),
        name="router",
    )(x, nw.reshape(1, d), mod, w_router_t, b_router.reshape(n_e, 1), tri)


def _dma_cparams(sem):
    return pltpu.CompilerParams(dimension_semantics=sem, vmem_limit_bytes=V7X_VMEM_LIMIT,
                                disable_bounds_checks=True)


def _dispatch_kernel(dest_ref, hp_ref, buf_ref, xb_ref, sem):
    del buf_ref
    tm = hp_ref.shape[0]

    for t in range(tm):
        for k in range(TOP_K):
            pltpu.make_async_copy(hp_ref.at[pl.ds(t, 1)], xb_ref.at[pl.ds(dest_ref[k, t], 1)], sem).start(
                priority=k % 2)
    for k in range(TOP_K):
        pltpu.make_async_copy(hp_ref, xb_ref.at[pl.ds(0, tm)], sem).wait()


def _dispatch(hp, dest, xb_buf):
    t, half = hp.shape
    tm = _pick(512, t)
    return pl.pallas_call(
        _dispatch_kernel,
        grid=(t // tm,),
        in_specs=[
            pl.BlockSpec((TOP_K, tm), lambda i: (0, i), memory_space=pltpu.SMEM),
            pl.BlockSpec((tm, half), lambda i: (i, 0)),
            pl.BlockSpec(memory_space=pl.ANY),
        ],
        out_specs=pl.BlockSpec(memory_space=pl.ANY),
        out_shape=jax.ShapeDtypeStruct(xb_buf.shape, xb_buf.dtype),
        input_output_aliases={2: 0},
        scratch_shapes=[pltpu.SemaphoreType.DMA],
        compiler_params=_dma_cparams(("arbitrary",)),
        name="dispatch",
    )(dest, hp, xb_buf)


def _expert_kernel(be_ref, nu_ref, nv_ref, xp_ref, wg_ref, bg_ref, wu_ref, bu_ref, wd_ref, bd_ref, o_ref, x_scr):
    i = pl.program_id(0)
    f = pl.program_id(1)
    half = xp_ref.shape[1]

    @pl.when(i >= nu_ref[0])
    def _():
        @pl.when(f == 0)
        def _():
            o_ref[...] = jnp.zeros(o_ref.shape, F32)

    @pl.when(i < nu_ref[0])
    def _():
        @pl.when(f == 0)
        def _():
            keep = lax.broadcasted_iota(I32, (xp_ref.shape[0], 1), 0) < nv_ref[i]
            lo, hi = _unpack_bf16_pairs(jnp.where(keep, xp_ref[...], jnp.uint32(0)))
            x_scr[:, :half] = lo
            x_scr[:, half:] = hi
            o_ref[...] = jnp.broadcast_to(bd_ref[0], o_ref.shape)

        x = x_scr[...]
        gate = jnp.minimum(_dot(x, wg_ref[...]) + bg_ref[0], SWIGLU_LIMIT)
        up = jnp.clip(_dot(x, wu_ref[...]) + bu_ref[0], -SWIGLU_LIMIT, SWIGLU_LIMIT)
        hdn = (up + 1.0) * (gate * jax.nn.sigmoid(SWIGLU_ALPHA * gate))
        o_ref[...] += _dot(hdn.astype(BF16), wd_ref[...].astype(BF16))


def _experts(xb, block_e, n_used, n_valid, layer, w_gate, b_gate, w_up, b_up, w_down, b_down, tm):
    cap, half = xb.shape
    d = 2 * half
    n_e, _, ff = w_gate.shape[1:]
    tf = 512
    nb = cap // tm
    cl = lambda i, nu: jnp.minimum(i, nu[0] - 1)
    wspec_in = pl.BlockSpec((None, None, d, tf), lambda i, f, be, nu, nv: (layer, be[cl(i, nu)], 0, f))
    wspec_out = pl.BlockSpec((None, None, tf, d), lambda i, f, be, nu, nv: (layer, be[cl(i, nu)], f, 0))
    bspec_in = pl.BlockSpec((1, 1, tf), lambda i, f, be, nu, nv: (layer * n_e + be[cl(i, nu)], 0, f))
    bspec_out = pl.BlockSpec((1, 1, d), lambda i, f, be, nu, nv: (layer * n_e + be[cl(i, nu)], 0, 0))
    grid_spec = pltpu.PrefetchScalarGridSpec(
        num_scalar_prefetch=3,
        grid=(nb, ff // tf),
        in_specs=[
            pl.BlockSpec((tm, half), lambda i, f, be, nu, nv: (cl(i, nu), 0)),
            wspec_in, bspec_in, wspec_in, bspec_in, wspec_out, bspec_out,
        ],
        out_specs=pl.BlockSpec((tm, d), lambda i, f, be, nu, nv: (i, 0)),
        scratch_shapes=[pltpu.VMEM((tm, d), BF16)],
    )
    return pl.pallas_call(
        _expert_kernel,
        grid_spec=grid_spec,
        out_shape=jax.ShapeDtypeStruct((cap, d), F32),
        compiler_params=_cparams(("parallel", "arbitrary")),
        name="experts",
    )(block_e, n_used, n_valid, xb, w_gate, b_gate.reshape(-1, 1, ff), w_up, b_up.reshape(-1, 1, ff),
      w_down, b_down.reshape(-1, 1, d))


def _combine_kernel(dcur_ref, dnxt_ref, w_ref, x_ref, mod_ref, yb_ref, o_ref, ybuf, sems, *, kg):
    i = pl.program_id(0)
    n = pl.num_programs(0)
    tm = x_ref.shape[0]
    slot = i % 2

    def fetch(d_ref, s):
        for t in range(tm):
            for k in range(TOP_K):
                pltpu.make_async_copy(yb_ref.at[pl.ds(d_ref[k, t], 1)], ybuf.at[s, k, pl.ds(t, 1)],
                                      sems.at[s]).start(priority=k % 2)

    @pl.when(i == 0)
    def _():
        fetch(dcur_ref, 0)

    for s in range(2):
        @pl.when(jnp.logical_and(i + 1 < n, slot == 1 - s))
        def _():
            fetch(dnxt_ref, s)

    for k in range(TOP_K):
        pltpu.make_async_copy(yb_ref.at[pl.ds(0, tm)], ybuf.at[slot, k], sems.at[slot]).wait()
    w = w_ref[...]
    f = ybuf[slot, 0] * w[:, 0:1]
    for k in range(1, TOP_K):
        f = f + ybuf[slot, k] * w[:, k:k + 1]
    o_ref[...] = x_ref[...] + mod_ref[0][kg:kg + 1, :] * f


def _combine(geo, yb, dest, w_tk, x, mod, kg):
    t, d = x.shape
    tm = _pick(128, geo.t_ctx, geo.seq_lat)
    nsteps = t // tm
    row = geo.mod_row(tm)
    return pl.pallas_call(
        functools.partial(_combine_kernel, kg=kg),
        grid=(nsteps,),
        in_specs=[
            pl.BlockSpec((TOP_K, tm), lambda i: (0, i), memory_space=pltpu.SMEM),
            pl.BlockSpec((TOP_K, tm), lambda i: (0, jnp.minimum(i + 1, nsteps - 1)), memory_space=pltpu.SMEM),
            pl.BlockSpec((tm, TOP_K), lambda i: (i, 0)),
            pl.BlockSpec((tm, d), lambda i: (i, 0)),
            pl.BlockSpec((1, 6, d), lambda i: (row(i), 0, 0)),
            pl.BlockSpec(memory_space=pl.ANY),
        ],
        out_specs=pl.BlockSpec((tm, d), lambda i: (i, 0)),
        out_shape=jax.ShapeDtypeStruct((t, d), F32),
        scratch_shapes=[pltpu.VMEM((2, TOP_K, tm, d), F32), pltpu.SemaphoreType.DMA((2,))],
        compiler_params=_dma_cparams(("arbitrary",)),
        name="combine",
    )(dest, dest, w_tk, x, mod, yb)


MOE_TM = 1024


def _moe_blocks(t, n_e):
    return -(-(t * TOP_K) // MOE_TM) + n_e


def _moe(geo, x, nw, mod, layer, w_router, b_router, w_gate, b_gate, w_up, b_up, w_down, b_down, xb_buf):
    t, d = x.shape
    n_e = w_router.shape[1]
    hp, top_e, top_w, rank, cnt = _router(geo, x, nw, mod, w_router.T, b_router, 3)
    counts = cnt[:, 0].astype(I32)
    padded = (counts + MOE_TM - 1) // MOE_TM * MOE_TM
    pad_end = jnp.cumsum(padded)
    pad_start = pad_end - padded
    nb = _moe_blocks(t, n_e)
    experts = jnp.arange(n_e, dtype=I32)
    start_of = jnp.sum(jnp.where(top_e[None] == experts[:, None, None], pad_start[:, None, None], 0), axis=0)
    dest = start_of + rank
    blk0 = jnp.arange(nb, dtype=I32) * MOE_TM
    block_e = jnp.minimum(jnp.sum((pad_end[None, :] <= blk0[:, None]).astype(I32), axis=1), n_e - 1)
    own = block_e[:, None] == experts[None, :]
    group_end = jnp.sum(jnp.where(own, (pad_start + counts)[None, :], 0), axis=1)
    n_valid = jnp.clip(group_end - blk0, 0, MOE_TM).astype(I32)
    n_used = (pad_end[-1:] // MOE_TM).astype(I32)
    xb = _dispatch(hp, dest, xb_buf)
    yb = _experts(xb, block_e, n_used, n_valid, layer, w_gate, b_gate, w_up, b_up, w_down, b_down, MOE_TM)
    return _combine(geo, yb, dest, top_w.T, x, mod, 5), xb


_SWAP = np.array([(j + ROPE_FREQ) if (j // ROPE_FREQ) % 2 == 0 else (j - ROPE_FREQ) for j in range(QK_ROPE)])


def _rope_table(seq_lat, tq):
    pos = jnp.arange(seq_lat)
    row = (pos // GRID_W).astype(F32)
    colp = (pos % GRID_W).astype(F32)
    inv = jnp.power(ROPE_THETA, -jnp.arange(ROPE_FREQ, dtype=F32) / ROPE_FREQ)
    ar, ac = row[:, None] * inv, colp[:, None] * inv
    cos = jnp.concatenate([jnp.cos(ar), jnp.cos(ar), jnp.cos(ac), jnp.cos(ac)], axis=1)
    sin = jnp.concatenate([-jnp.sin(ar), jnp.sin(ar), -jnp.sin(ac), jnp.sin(ac)], axis=1)
    ident = jnp.concatenate([jnp.ones((tq, QK_ROPE), F32), jnp.zeros((tq, QK_ROPE), F32)], axis=1)
    return jnp.concatenate([ident, jnp.concatenate([cos, sin], axis=1)], axis=0)


def _even_params(w_in, w_qb, q_norm, q_rope_norm, k_rope_norm):
    d = w_in.shape[0]
    o_kv, o_kr, o_conv = Q_LORA, Q_LORA + KV_LORA, Q_LORA + KV_LORA + QK_ROPE
    kr = w_in[:, o_kr:o_conv]
    pad = jnp.zeros((d, EV_N - EV_KR - 2 * QK_ROPE), w_in.dtype)
    w_in_r = jnp.concatenate([w_in[:, o_conv:], w_in[:, :o_kr], kr, kr[:, _SWAP], pad], axis=1).astype(BF16)
    wq = w_qb.reshape(Q_LORA, MLA_HEADS, QK_NOPE + QK_ROPE)
    wq_r = jnp.concatenate([wq, wq[:, :, QK_NOPE:][:, :, _SWAP]], axis=2).reshape(Q_LORA, MLA_HEADS * 256)
    q_nw = jnp.concatenate([q_norm, q_rope_norm, q_rope_norm[_SWAP]]).reshape(1, 256)
    kr_nw = jnp.concatenate([k_rope_norm, k_rope_norm[_SWAP]]).reshape(1, 2 * QK_ROPE)
    return w_in_r, wq_r.astype(BF16), q_nw, kr_nw


def _odd_params(w_in, w_gk, b_gk):
    d = w_in.shape[0]
    pad = jnp.zeros((d, OD_N - w_in.shape[1]), w_in.dtype)
    w_in_r = jnp.concatenate([w_in, pad], axis=1).astype(BF16)
    wpad = jnp.zeros((2, LANES, w_gk.shape[2]), F32)
    wpad = wpad.at[0, :GATE_RANK].set(w_gk[0]).at[1, GATE_RANK:2 * GATE_RANK].set(w_gk[1])
    return w_in_r, wpad.astype(BF16), b_gk.reshape(2, 1, -1)


def kernel(x_prompt, x_sample, cache_mla_ckv, cache_mla_krope, state_gla, c, c_ctx, ada_w, ada_b, norm_mix_w, norm_ffn_w, ev_w_in, ev_q_a_norm, ev_w_qb, ev_kv_a_norm, ev_w_kvb, ev_q_norm, ev_k_norm, ev_q_rope_norm, ev_k_rope_norm, ev_conv_w, ev_w_out, od_w_in, od_w_gk, od_b_gk, od_o_norm, od_w_out, moe_w_router, moe_b_router, moe_w_gate, moe_b_gate, moe_w_up, moe_b_up, moe_w_down, moe_b_down):
    geo = _Geo(x_prompt, x_sample, cache_mla_ckv)
    d = geo.d
    depth = ada_w.shape[0]
    x = jnp.concatenate([x_prompt.reshape(geo.t_ctx, d), x_sample.reshape(geo.t_lat, d)], axis=0)
    cond = jnp.concatenate([c_ctx[None, :], c, jnp.zeros((16 - 1 - geo.n_lat, d), F32)], axis=0)
    mods = _adaln(cond, ada_w, ada_b).reshape(depth, 16, 6, d)
    tq = _pick(512, geo.t_ctx, geo.seq_lat)
    tab = _rope_table(geo.seq_lat, tq)
    w_gate, w_up, w_down = moe_w_gate.astype(BF16), moe_w_up.astype(BF16), moe_w_down
    xb_buf = jnp.zeros((_moe_blocks(geo.t, moe_w_router.shape[2]) * MOE_TM, d // 2), U32)
    attn_buf = jnp.zeros((geo.t, MLA_HEADS * V_HEAD), BF16)
    gla_bufs = (jnp.zeros((geo.t, GLA_HEADS * GLA_DVH), BF16), jnp.zeros((geo.t, GLA_HEADS * GLA_DVH), BF16))
    new_ckv, new_krope, new_gla = [], [], []
    for l in range(depth):
        i = l // 2
        mod = mods[l]
        if l % 2 == 0:
            w_in_r, wq_r, q_nw, kr_nw = _even_params(ev_w_in[i], ev_w_qb[i], ev_q_norm[i], ev_q_rope_norm[i],
                                                     ev_k_rope_norm[i])
            z = _lnmod_mm(geo, x, norm_mix_w[l], mod, w_in_r, 0)
            conv, qan, ckv, krn, krot = _even_mid(geo, z, ev_conv_w[i], ev_q_a_norm[i], ev_kv_a_norm[i], kr_nw,
                                                  tab, tq)
            q256 = _q_proj(geo, qan, wq_r, q_nw, tab, tq)
            kc = cache_mla_krope[:, i].reshape(geo.n_lat * geo.past, QK_ROPE)
            ckv_all = jnp.concatenate([ckv, cache_mla_ckv[:, i].reshape(geo.n_lat * geo.past, KV_LORA)], axis=0)
            krot_all = jnp.concatenate([krot, jnp.concatenate([kc, kc], axis=1).astype(BF16)], axis=0)
            k256, v256 = _kv_proj(ckv_all, ev_w_kvb[i].astype(BF16), ev_k_norm[i], krot_all)
            attn_buf = _attention(geo, q256, k256, v256, attn_buf)
            x = _mix_out(geo, attn_buf, conv, ev_w_out[i].astype(BF16), x, mod, 2)
            new_ckv.append(ckv[:geo.t_ctx].reshape(geo.n_ctx, geo.seq_ctx, KV_LORA))
            new_krope.append(krn[:geo.t_ctx].reshape(geo.n_ctx, geo.seq_ctx, QK_ROPE))
        else:
            w_in_r, wgk_pad, bgk = _odd_params(od_w_in[i], od_w_gk[i], od_b_gk[i])
            z = _lnmod_mm(geo, x, norm_mix_w[l], mod, w_in_r, 0)
            o_f, o_b, s_fin = _gla_scan(z, wgk_pad, bgk, geo.n_ctx, geo.seq_ctx, 0, None, True, gla_bufs)
            gla_bufs = _gla_scan(z, wgk_pad, bgk, geo.n_lat, geo.seq_lat, geo.t_ctx, state_gla[:, i], False,
                                 (o_f, o_b))
            x = _gla_out(geo, gla_bufs[0], gla_bufs[1], z, od_o_norm[i], od_w_out[i].astype(BF16), x, mod, 2)
            new_gla.append(s_fin)
        x, xb_buf = _moe(geo, x, norm_ffn_w[l], mod, l, moe_w_router[l], moe_b_router[l], w_gate, moe_b_gate,
                         w_up, moe_b_up, w_down, moe_b_down, xb_buf)
    xp = x[:geo.t_ctx].reshape(x_prompt.shape)
    xs = x[geo.t_ctx:].reshape(x_sample.shape)
    return (xp, xs, jnp.stack(new_ckv, axis=1), jnp.stack(new_krope, axis=1), jnp.stack(new_gla, axis=1))
```
